```python
import jax, jax.numpy as jnp
from jax import lax
import numpy as np

D_MODEL = 2048
BATCH = 4
SEQ = 2048
DEPTH = 1
DEC_BATCH = 128
DEC_SEQ = 4
PAST_LEN = 16384
PAGE_SIZE = 128

N_HEADS_A = 4
DK_A = 256
DV_A = 512
QK_A = N_HEADS_A * DK_A
V_A = N_HEADS_A * DV_A
CONV_W = 4
MLSTM_CHUNK = 64
N_GROUPS_B = 4
D_B = 2048
GMLP_CHUNK = 128
D_FF = 5632
D_PLE = 256
EPS = 1e-6
D_IN = 2 * QK_A + 2 * V_A + 2 * N_HEADS_A + 2 * D_B + 2 * D_MODEL

kernel_name = 'hybrid_mlstm_gmlp_decoder_step'


def _rms(x, g):
    xf = x.astype(jnp.float32)
    y = xf * lax.rsqrt(jnp.mean(xf * xf, axis=-1, keepdims=True) + EPS)
    return (y * g.astype(jnp.float32)).astype(x.dtype)


def _layer_norm(x, g, b):
    xf = x.astype(jnp.float32)
    mu = jnp.mean(xf, axis=-1, keepdims=True)
    var = jnp.mean(jnp.square(xf - mu), axis=-1, keepdims=True)
    y = (xf - mu) * lax.rsqrt(var + EPS) * g.astype(jnp.float32) + b.astype(jnp.float32)
    return y.astype(x.dtype)


def _swiglu(x, wg, wu, wd):
    return (jax.nn.silu(x @ wg) * (x @ wu)) @ wd


def _causal_conv(x, buf, w, b):
    S = x.shape[1]
    xp = jnp.concatenate([buf.astype(x.dtype), x], axis=1)
    y = b
    for j in range(CONV_W):
        y = y + xp[:, j:j + S] * w[j]
    return y, xp[:, S:]


def _mlstm(q, k, v, ig, lf, C0, n0, m0):
    B, S, H, _ = q.shape
    L = min(S, MLSTM_CHUNK)
    nc = S // L

    def to_chunks(a):
        return jnp.moveaxis(a.reshape((B, nc, L) + a.shape[2:]), 1, 0)

    xs = (to_chunks(q), to_chunks(k), to_chunks(v), to_chunks(ig), to_chunks(lf))
    mask = jnp.tril(jnp.ones((L, L), dtype=bool))

    def step(carry, inp):
        C, n, m = carry
        qc, kc, vc, ic, fc = inp
        qc = qc.transpose(0, 2, 1, 3)
        kc = kc.transpose(0, 2, 1, 3)
        vc = vc.transpose(0, 2, 1, 3)
        ic = ic.transpose(0, 2, 1)
        bcum = jnp.cumsum(fc.transpose(0, 2, 1), axis=-1)
        d = jnp.where(mask, bcum[..., :, None] - bcum[..., None, :] + ic[..., None, :], -jnp.inf)
        m_in = bcum + m[..., None]
        m_t = jnp.maximum(m_in, jnp.max(d, axis=-1))
        s = jnp.einsum('bhtd,bhsd->bhts', qc, kc) * jnp.exp(d - m_t[..., None])
        w_prev = jnp.exp(m_in - m_t)
        num = jnp.einsum('bhts,bhsv->bhtv', s, vc) + w_prev[..., None] * jnp.einsum('bhtd,bhdv->bhtv', qc, C)
        den = jnp.sum(s, axis=-1) + w_prev * jnp.einsum('bhtd,bhd->bht', qc, n)
        h = num / jnp.maximum(jnp.abs(den), jnp.exp(-m_t))[..., None]
        m_new = m_t[..., -1]
        w_end = jnp.exp(bcum[..., -1:] - bcum + ic - m_new[..., None])
        decay = jnp.exp(bcum[..., -1] + m - m_new)
        C_new = decay[..., None, None] * C + jnp.einsum('bhs,bhsd,bhsv->bhdv', w_end, kc, vc)
        n_new = decay[..., None] * n + jnp.einsum('bhs,bhsd->bhd', w_end, kc)
        return (C_new, n_new, m_new), h

    (C, n, m), hs = lax.scan(step, (C0, n0, m0), xs)
    h = jnp.moveaxis(hs, 0, 1).transpose(0, 1, 3, 2, 4).reshape(B, S, H, -1)
    return h, C, n, m


def _spatial_gate(v, w_s, b_s):
    B, S, _ = v.shape
    L = min(S, GMLP_CHUNK)
    vc = v.reshape(B, S // L, L, N_GROUPS_B, D_B // N_GROUPS_B)
    w = jnp.where(jnp.tril(jnp.ones((L, L), dtype=bool)), w_s[:, :L, :L], 0.0).astype(v.dtype)
    out = jnp.einsum('gts,bnsgc->bntgc', w, vc) + b_s[:, :L].T[None, None, :, :, None].astype(v.dtype)
    return out.reshape(B, S, D_B)


def _layer(x, p, conv_buf, C0, n0, m0, lw):
    B, S, _ = x.shape
    f32 = jnp.float32
    h = x + 0.5 * _rms(_swiglu(_rms(x, lw['g_ffn1_pre']), lw['w_ffn1_gate'], lw['w_ffn1_up'], lw['w_ffn1_down']), lw['g_ffn1_post'])
    xn = _rms(h, lw['g_mix_pre'])
    z = xn @ lw['w_in']
    sizes = [2 * QK_A, V_A, V_A, N_HEADS_A, N_HEADS_A, D_B, D_B, D_MODEL, D_MODEL]
    qk_pre, v_a, o_a, i_pre, f_pre, u_b, v_b, gate_a, gate_b = jnp.split(z, [int(c) for c in np.cumsum(sizes)[:-1]], axis=-1)
    qk, new_buf = _causal_conv(qk_pre, conv_buf, lw['w_conv'], lw['b_conv'])
    qk = jax.nn.silu(qk).astype(f32)
    q = qk[..., :QK_A].reshape(B, S, N_HEADS_A, DK_A)
    k = qk[..., QK_A:].reshape(B, S, N_HEADS_A, DK_A) * (DK_A ** -0.5)
    v = v_a.astype(f32).reshape(B, S, N_HEADS_A, DV_A)
    ig = i_pre.astype(f32) + lw['b_igate'].astype(f32)
    lf = jax.nn.log_sigmoid(f_pre.astype(f32) + lw['b_fgate'].astype(f32))
    h_a, C, n, m = _mlstm(q, k, v, ig, lf, C0.astype(f32), n0.astype(f32), m0.astype(f32))
    h_a = h_a * lax.rsqrt(jnp.mean(h_a * h_a, axis=-1, keepdims=True) + EPS) * lw['g_head'].astype(f32).reshape(N_HEADS_A, DV_A)
    h_a = (jax.nn.sigmoid(o_a.astype(f32)) * h_a.reshape(B, S, V_A)).astype(x.dtype)
    y_a = h_a @ lw['w_a_out']
    u = jax.nn.gelu(u_b, approximate=False)
    vg = _layer_norm(jax.nn.gelu(v_b, approximate=False), lw['g_ln_v'], lw['b_ln_v'])
    y_b = (u * _spatial_gate(vg, lw['w_spatial'], lw['b_spatial'])) @ lw['w_b_out']
    mix = (jax.nn.sigmoid(gate_a) * y_a + jax.nn.sigmoid(gate_b) * y_b) @ lw['w_o']
    h = h + _rms(mix, lw['g_mix_post'])
    h = h + 0.5 * _rms(_swiglu(_rms(h, lw['g_ffn2_pre']), lw['w_ffn2_gate'], lw['w_ffn2_up'], lw['w_ffn2_down']), lw['g_ffn2_post'])
    e = jax.nn.sigmoid(_rms(h, lw['g_ple_pre']) @ lw['w_ple_gate']) * (p @ lw['w_ple_up'])
    h = h + _rms(e, lw['g_ple_post'])
    return h, new_buf, C, n, m, vg


def setup_inputs(seed: int = 0) -> dict:
    key = jax.random.key(seed)
    ks = list(jax.random.split(key, 48))

    def nrm(shape, scale):
        return jax.random.normal(ks.pop(), shape, jnp.float32) * scale

    def gain(width):
        return 1.0 + nrm((DEPTH, width), 0.05)

    return {
        'x_prompt': nrm((BATCH, SEQ, D_MODEL), 1.0),
        'x_sample': nrm((DEC_BATCH, DEC_SEQ, D_MODEL), 1.0),
        'p_prompt': nrm((DEPTH, BATCH, SEQ, D_PLE), 1.0),
        'p_sample': nrm((DEPTH, DEC_BATCH, DEC_SEQ, D_PLE), 1.0),
        'state_mlstm_conv': nrm((DEPTH, DEC_BATCH, CONV_W - 1, 2 * QK_A), 1.0),
        'state_mlstm_C': nrm((DEPTH, DEC_BATCH, N_HEADS_A, DK_A, DV_A), 0.02),
        'state_mlstm_n': nrm((DEPTH, DEC_BATCH, N_HEADS_A, DK_A), 0.1),
        'state_mlstm_m': nrm((DEPTH, DEC_BATCH, N_HEADS_A), 1.0),
        'g_ffn1_pre': gain(D_MODEL),
        'w_ffn1_gate': nrm((DEPTH, D_MODEL, D_FF), D_MODEL ** -0.5),
        'w_ffn1_up': nrm((DEPTH, D_MODEL, D_FF), D_MODEL ** -0.5),
        'w_ffn1_down': nrm((DEPTH, D_FF, D_MODEL), D_FF ** -0.5),
        'g_ffn1_post': gain(D_MODEL),
        'g_mix_pre': gain(D_MODEL),
        'w_in': nrm((DEPTH, D_MODEL, D_IN), D_MODEL ** -0.5),
        'w_conv': nrm((DEPTH, CONV_W, 2 * QK_A), CONV_W ** -0.5),
        'b_conv': nrm((DEPTH, 2 * QK_A), 0.02),
        'b_igate': nrm((DEPTH, N_HEADS_A), 0.1),
        'b_fgate': 3.0 + nrm((DEPTH, N_HEADS_A), 0.5),
        'g_head': gain(V_A),
        'w_a_out': nrm((DEPTH, V_A, D_MODEL), V_A ** -0.5),
        'g_ln_v': gain(D_B),
        'b_ln_v': nrm((DEPTH, D_B), 0.02),
        'w_spatial': nrm((DEPTH, N_GROUPS_B, GMLP_CHUNK, GMLP_CHUNK), GMLP_CHUNK ** -0.5),
        'b_spatial': 1.0 + nrm((DEPTH, N_GROUPS_B, GMLP_CHUNK), 0.05),
        'w_b_out': nrm((DEPTH, D_B, D_MODEL), D_B ** -0.5),
        'w_o': nrm((DEPTH, D_MODEL, D_MODEL), D_MODEL ** -0.5),
        'g_mix_post': gain(D_MODEL),
        'g_ffn2_pre': gain(D_MODEL),
        'w_ffn2_gate': nrm((DEPTH, D_MODEL, D_FF), D_MODEL ** -0.5),
        'w_ffn2_up': nrm((DEPTH, D_MODEL, D_FF), D_MODEL ** -0.5),
        'w_ffn2_down': nrm((DEPTH, D_FF, D_MODEL), D_FF ** -0.5),
        'g_ffn2_post': gain(D_MODEL),
        'g_ple_pre': gain(D_MODEL),
        'w_ple_gate': nrm((DEPTH, D_MODEL, D_MODEL), D_MODEL ** -0.5),
        'w_ple_up': nrm((DEPTH, D_PLE, D_MODEL), D_PLE ** -0.5),
        'g_ple_post': gain(D_MODEL),
    }


def reference(x_prompt, x_sample, p_prompt, p_sample, state_mlstm_conv, state_mlstm_C, state_mlstm_n, state_mlstm_m,
              g_ffn1_pre, w_ffn1_gate, w_ffn1_up, w_ffn1_down, g_ffn1_post,
              g_mix_pre, w_in, w_conv, b_conv, b_igate, b_fgate, g_head, w_a_out,
              g_ln_v, b_ln_v, w_spatial, b_spatial, w_b_out, w_o, g_mix_post,
              g_ffn2_pre, w_ffn2_gate, w_ffn2_up, w_ffn2_down, g_ffn2_post,
              g_ple_pre, w_ple_gate, w_ple_up, g_ple_post):
    hp, hs = x_prompt, x_sample
    conv_p, C_p, n_p, m_p = [], [], [], []
    conv_s, C_s, n_s, m_s, v_s = [], [], [], [], []
    for i in range(DEPTH):
        lw = dict(g_ffn1_pre=g_ffn1_pre[i], w_ffn1_gate=w_ffn1_gate[i], w_ffn1_up=w_ffn1_up[i], w_ffn1_down=w_ffn1_down[i],
                  g_ffn1_post=g_ffn1_post[i], g_mix_pre=g_mix_pre[i], w_in=w_in[i], w_conv=w_conv[i], b_conv=b_conv[i],
                  b_igate=b_igate[i], b_fgate=b_fgate[i], g_head=g_head[i], w_a_out=w_a_out[i], g_ln_v=g_ln_v[i],
                  b_ln_v=b_ln_v[i], w_spatial=w_spatial[i], b_spatial=b_spatial[i], w_b_out=w_b_out[i], w_o=w_o[i],
                  g_mix_post=g_mix_post[i], g_ffn2_pre=g_ffn2_pre[i], w_ffn2_gate=w_ffn2_gate[i], w_ffn2_up=w_ffn2_up[i],
                  w_ffn2_down=w_ffn2_down[i], g_ffn2_post=g_ffn2_post[i], g_ple_pre=g_ple_pre[i],
                  w_ple_gate=w_ple_gate[i], w_ple_up=w_ple_up[i], g_ple_post=g_ple_post[i])
        buf0 = jnp.zeros((hp.shape[0], CONV_W - 1, 2 * QK_A), hp.dtype)
        C0 = jnp.zeros((hp.shape[0], N_HEADS_A, DK_A, DV_A), jnp.float32)
        n0 = jnp.zeros((hp.shape[0], N_HEADS_A, DK_A), jnp.float32)
        m0 = jnp.zeros((hp.shape[0], N_HEADS_A), jnp.float32)
        hp, bp, cp, np_, mp, _ = _layer(hp, p_prompt[i], buf0, C0, n0, m0, lw)
        conv_p.append(bp); C_p.append(cp); n_p.append(np_); m_p.append(mp)
        hs, bs, cs, ns, ms, vs = _layer(hs, p_sample[i], state_mlstm_conv[i], state_mlstm_C[i], state_mlstm_n[i], state_mlstm_m[i], lw)
        conv_s.append(bs); C_s.append(cs); n_s.append(ns); m_s.append(ms); v_s.append(vs)
    return (hp, hs, jnp.stack(conv_p), jnp.stack(C_p), jnp.stack(n_p), jnp.stack(m_p),
            jnp.stack(conv_s), jnp.stack(C_s), jnp.stack(n_s), jnp.stack(m_s), jnp.stack(v_s))
```

```python
import functools

import jax
import jax.numpy as jnp
from jax import lax
from jax.experimental import pallas as pl
from jax.experimental.pallas import tpu as pltpu

F32 = jnp.float32
BF16 = jnp.bfloat16
EPS = 1e-6

N_HEADS = 4
DK = 256
DV = 512
CONV_W = 4
N_GROUPS = 4
GMLP_CHUNK = 128
LANES = 128
SUBLANES = 8
MLSTM_CHUNK = 256
SAMPLE_PAD = 16
VMEM_LIMIT = 54 * 1024 * 1024


def _params(*sem):
    return pltpu.CompilerParams(dimension_semantics=sem, vmem_limit_bytes=VMEM_LIMIT)


def _rms(x, g):
    return x * lax.rsqrt(jnp.mean(x * x, axis=-1, keepdims=True) + EPS) * g


def _gelu(x):
    return 0.5 * x * (1.0 + lax.erf(x * 0.7071067811865476))


def _sigmoid(x):
    return jax.nn.sigmoid(x)


def _identity(x):
    return x


def _resident(shape):
    return pl.BlockSpec(shape, lambda *_: (0,) * len(shape), pipeline_mode=pl.Buffered(1))


def _rms_cast_kernel(x_ref, g_ref, o_ref):
    o_ref[...] = _rms(x_ref[...], g_ref[...]).astype(o_ref.dtype)


def _rms_cast(x, g, tm=512):
    m, d = x.shape
    return pl.pallas_call(
        _rms_cast_kernel,
        grid=(m // tm,),
        in_specs=[pl.BlockSpec((tm, d), lambda i: (i, 0)),
                  pl.BlockSpec((1, d), lambda i: (0, 0))],
        out_specs=pl.BlockSpec((tm, d), lambda i: (i, 0)),
        out_shape=jax.ShapeDtypeStruct((m, d), BF16),
        compiler_params=_params("arbitrary"),
        name="rms_cast",
    )(x, g)


def _ffn_up_kernel(x_ref, wg_ref, wu_ref, o_ref, wgb, wub):
    @pl.when(pl.program_id(1) == 0)
    def _():
        wgb[...] = wg_ref[...].astype(BF16)
        wub[...] = wu_ref[...].astype(BF16)

    x = x_ref[...]
    g = jnp.dot(x, wgb[...], preferred_element_type=F32)
    u = jnp.dot(x, wub[...], preferred_element_type=F32)
    o_ref[...] = (g * _sigmoid(g) * u).astype(o_ref.dtype)


def _ffn_up(xn, wg, wu, tm=512, tf=512):
    m, d = xn.shape
    f = wg.shape[1]
    return pl.pallas_call(
        _ffn_up_kernel,
        grid=(f // tf, m // tm),
        in_specs=[pl.BlockSpec((tm, d), lambda j, i: (i, 0)),
                  pl.BlockSpec((d, tf), lambda j, i: (0, j)),
                  pl.BlockSpec((d, tf), lambda j, i: (0, j))],
        out_specs=pl.BlockSpec((tm, tf), lambda j, i: (i, j)),
        out_shape=jax.ShapeDtypeStruct((m, f), BF16),
        scratch_shapes=[pltpu.VMEM((d, tf), BF16), pltpu.VMEM((d, tf), BF16)],
        compiler_params=_params("arbitrary", "arbitrary"),
        name="ffn_up",
    )(xn, wg, wu)


def _mm_kernel(x_ref, w_ref, o_ref, wb, *, acts, tiles_per_act):
    @pl.when(pl.program_id(1) == 0)
    def _():
        wb[...] = w_ref[...].astype(BF16)

    y = jnp.dot(x_ref[...], wb[...], preferred_element_type=F32)
    if len(acts) == 1:
        o_ref[...] = acts[0](y).astype(o_ref.dtype)
    else:
        seg = pl.program_id(0) // tiles_per_act
        for a, act in enumerate(acts):
            @pl.when(seg == a)
            def _(act=act):
                o_ref[...] = act(y).astype(o_ref.dtype)


def _mm(x, w, col_off, n_cols, acts, out_dtype, tm=512, tn=512):
    m, k = x.shape
    tn = min(tn, n_cols)
    n_tiles = n_cols // tn
    off = col_off // tn
    assert off * tn == col_off and n_tiles * tn == n_cols and n_tiles % len(acts) == 0
    kern = functools.partial(_mm_kernel, acts=tuple(acts), tiles_per_act=n_tiles // len(acts))
    return pl.pallas_call(
        kern,
        grid=(n_tiles, m // tm),
        in_specs=[pl.BlockSpec((tm, k), lambda j, i: (i, 0)),
                  pl.BlockSpec((k, tn), lambda j, i: (0, off + j))],
        out_specs=pl.BlockSpec((tm, tn), lambda j, i: (i, j)),
        out_shape=jax.ShapeDtypeStruct((m, n_cols), out_dtype),
        scratch_shapes=[pltpu.VMEM((k, tn), BF16)],
        compiler_params=_params("arbitrary", "arbitrary"),
        name="mm",
    )(x, w)


def _ffn_down_kernel(h_ref, wd_ref, x_ref, gpost_ref, gnext_ref, hout_ref, xn_ref):
    y = jnp.dot(h_ref[...], wd_ref[...], preferred_element_type=F32)
    h = x_ref[...] + 0.5 * _rms(y, gpost_ref[...])
    hout_ref[...] = h
    xn_ref[...] = _rms(h, gnext_ref[...]).astype(xn_ref.dtype)


def _ffn_down(hid, wd, x, g_post, g_next, tm=256):
    m, f = hid.shape
    d = wd.shape[1]
    row = lambda i: (i, 0)
    return pl.pallas_call(
        _ffn_down_kernel,
        grid=(m // tm,),
        in_specs=[pl.BlockSpec((tm, f), row), _resident((f, d)),
                  pl.BlockSpec((tm, d), row), _resident((1, d)), _resident((1, d))],
        out_specs=[pl.BlockSpec((tm, d), row), pl.BlockSpec((tm, d), row)],
        out_shape=[jax.ShapeDtypeStruct((m, d), F32), jax.ShapeDtypeStruct((m, d), BF16)],
        compiler_params=_params("arbitrary"),
        name="ffn_down",
    )(hid, wd, x, g_post, g_next)


def _merge_kernel(ha_ref, sg_ref, gates_a_ref, gates_b_ref, h_ref, wa_ref, wb_ref, wo_ref,
                  gpost_ref, gnext_ref, hout_ref, xn_ref):
    ya = jnp.dot(ha_ref[...], wa_ref[...], preferred_element_type=F32)
    yb = jnp.dot(sg_ref[...], wb_ref[...], preferred_element_type=F32)
    mixin = (gates_a_ref[...] * ya + gates_b_ref[...] * yb).astype(BF16)
    mix = jnp.dot(mixin, wo_ref[...], preferred_element_type=F32)
    h = h_ref[...] + _rms(mix, gpost_ref[...])
    hout_ref[...] = h
    xn_ref[...] = _rms(h, gnext_ref[...]).astype(xn_ref.dtype)


def _merge(ha, sg, tail, h, wa, wb, wo, g_post, g_next, tm=256):
    m, d = h.shape
    row = lambda i: (i, 0)
    return pl.pallas_call(
        _merge_kernel,
        grid=(m // tm,),
        in_specs=[pl.BlockSpec((tm, d), row), pl.BlockSpec((tm, d), row),
                  pl.BlockSpec((tm, d), lambda i: (i, 2)), pl.BlockSpec((tm, d), lambda i: (i, 3)),
                  pl.BlockSpec((tm, d), row),
                  _resident((d, d)), _resident((d, d)), _resident((d, d)),
                  _resident((1, d)), _resident((1, d))],
        out_specs=[pl.BlockSpec((tm, d), row), pl.BlockSpec((tm, d), row)],
        out_shape=[jax.ShapeDtypeStruct((m, d), F32), jax.ShapeDtypeStruct((m, d), BF16)],
        compiler_params=_params("arbitrary"),
        name="merge",
    )(ha, sg, tail, tail, h, wa, wb, wo, g_post, g_next)


def _ple_kernel(xn_ref, p_ref, h_ref, wg_ref, wu_ref, gpost_ref, out_ref):
    gate = _sigmoid(jnp.dot(xn_ref[...], wg_ref[...], preferred_element_type=F32))
    up = jnp.dot(p_ref[...].astype(BF16), wu_ref[...], preferred_element_type=F32)
    out_ref[...] = h_ref[...] + _rms(gate * up, gpost_ref[...])


def _ple(xn, p, h, wg, wu, g_post, tm=256):
    m, d = h.shape
    dp = p.shape[1]
    row = lambda i: (i, 0)
    return pl.pallas_call(
        _ple_kernel,
        grid=(m // tm,),
        in_specs=[pl.BlockSpec((tm, d), row), pl.BlockSpec((tm, dp), row), pl.BlockSpec((tm, d), row),
                  _resident((d, d)), _resident((dp, d)), _resident((1, d))],
        out_specs=pl.BlockSpec((tm, d), row),
        out_shape=jax.ShapeDtypeStruct((m, d), F32),
        compiler_params=_params("arbitrary"),
        name="ple",
    )(xn, p, h, wg, wu, g_post)


def _conv_kernel(x_ref, p0_ref, w_ref, b_ref, q_ref, k_ref, prev, *, ts):
    @pl.when(pl.program_id(1) == 0)
    def _():
        prev[...] = p0_ref[...]

    x = x_ref[...]
    tail = prev[...]
    w = w_ref[...]
    c = x.shape[1]
    y = b_ref[...] + x * w[CONV_W - 1:CONV_W, :]
    rowid = lax.broadcasted_iota(jnp.int32, (SUBLANES, c), 0)
    for j in range(1, CONV_W):
        xr = pltpu.roll(x, j, axis=0)
        tr = pltpu.roll(tail, j, axis=0)
        top = jnp.where(rowid < j, tr, xr[0:SUBLANES])
        xs = jnp.concatenate([top, xr[SUBLANES:]], axis=0)
        y = y + xs * w[CONV_W - 1 - j:CONV_W - j, :]
    prev[...] = x[ts - SUBLANES:ts]
    s = y * _sigmoid(y)
    half = c // 2
    q_ref[...] = s[:, :half].astype(q_ref.dtype)
    k_ref[...] = (s[:, half:] * (DK ** -0.5)).astype(k_ref.dtype)


def _conv(x, p0, w, b, nb, seq, ts):
    c = x.shape[1]
    nt = seq // ts
    kern = functools.partial(_conv_kernel, ts=ts)
    return pl.pallas_call(
        kern,
        grid=(nb, nt),
        in_specs=[pl.BlockSpec((ts, c), lambda b_, t: (b_ * nt + t, 0)),
                  pl.BlockSpec((SUBLANES, c), lambda b_, t: (b_, 0)),
                  pl.BlockSpec((CONV_W, c), lambda b_, t: (0, 0)),
                  pl.BlockSpec((1, c), lambda b_, t: (0, 0))],
        out_specs=[pl.BlockSpec((ts, c // 2), lambda b_, t: (b_ * nt + t, 0)),
                   pl.BlockSpec((ts, c // 2), lambda b_, t: (b_ * nt + t, 0))],
        out_shape=[jax.ShapeDtypeStruct((nb * seq, c // 2), BF16),
                   jax.ShapeDtypeStruct((nb * seq, c // 2), BF16)],
        scratch_shapes=[pltpu.VMEM((SUBLANES, c), F32)],
        compiler_params=_params("arbitrary", "arbitrary"),
        name="conv",
    )(x, p0, w, b)


def _mlstm_kernel(*refs, chunk, n_valid, has_init):
    if has_init:
        (q_ref, k_ref, v_ref, o_ref, g_ref, gb_ref, gh_ref, c0_ref, n0_ref, m0_ref,
         h_out, c_out, n_out, m_out, c_s, n_s, m_s) = refs
    else:
        (q_ref, k_ref, v_ref, o_ref, g_ref, gb_ref, gh_ref,
         h_out, c_out, n_out, m_out, c_s, n_s, m_s) = refs
    L = chunk
    head = pl.program_id(1)
    ci = pl.program_id(2)

    @pl.when(ci == 0)
    def _():
        if has_init:
            c_s[...] = c0_ref[...]
            n_s[...] = n0_ref[...]
            m_s[...] = m0_ref[...]
        else:
            c_s[...] = jnp.zeros_like(c_s)
            n_s[...] = jnp.zeros_like(n_s)
            m_s[...] = jnp.zeros_like(m_s)

    g = g_ref[...] + gb_ref[...]
    lane = lax.broadcasted_iota(jnp.int32, g.shape, 1)
    ig = jnp.sum(jnp.where(lane == head, g, 0.0), axis=1, keepdims=True)
    fg = jnp.sum(jnp.where(lane == head + N_HEADS, g, 0.0), axis=1, keepdims=True)
    lf = jnp.minimum(fg, 0.0) - jnp.log1p(jnp.exp(-jnp.abs(fg)))
    if n_valid < L:
        rcol = lax.broadcasted_iota(jnp.int32, (L, 1), 0)
        ig = jnp.where(rcol < n_valid, ig, -jnp.inf)
        lf = jnp.where(rcol < n_valid, lf, 0.0)

    bc = jnp.broadcast_to(lf, (L, LANES))
    rowi = lax.broadcasted_iota(jnp.int32, (L, LANES), 0)
    sh = 1
    while sh < L:
        bc = bc + jnp.where(rowi >= sh, pltpu.roll(bc, sh, axis=0), 0.0)
        sh *= 2
    bcum = bc[:, 0:1]

    m_prev = m_s[0:1, 0:1]
    m_in = bcum + m_prev
    a_col = ig - bcum
    r_i = lax.broadcasted_iota(jnp.int32, (L, L), 0)
    c_i = lax.broadcasted_iota(jnp.int32, (L, L), 1)
    a_row = jnp.sum(jnp.where(r_i == c_i, jnp.broadcast_to(a_col, (L, L)), 0.0),
                    axis=0, keepdims=True)
    d = jnp.where(r_i >= c_i, bcum + a_row, -jnp.inf)
    m_t = jnp.maximum(m_in, jnp.max(d, axis=1, keepdims=True))
    p = jnp.exp(d - m_t)

    qb = q_ref[...]
    kb = k_ref[...]
    vb = v_ref[...]
    s = lax.dot_general(qb, kb, (((1,), (1,)), ((), ())), preferred_element_type=F32) * p
    w_prev = jnp.exp(m_in - m_t)
    cb = c_s[...].astype(BF16)
    num = (jnp.dot(s.astype(BF16), vb, preferred_element_type=F32)
           + w_prev * jnp.dot(qb, cb, preferred_element_type=F32))
    den = (jnp.sum(s, axis=1, keepdims=True)
           + w_prev * jnp.sum(qb.astype(F32) * n_s[...], axis=1, keepdims=True))
    hh = num / jnp.maximum(jnp.abs(den), jnp.exp(-m_t))

    hn = hh * lax.rsqrt(jnp.mean(hh * hh, axis=1, keepdims=True) + EPS) * gh_ref[...]
    h_out[...] = (o_ref[...] * hn).astype(h_out.dtype)

    m_new = m_t[L - 1:L, :]
    b_last = bcum[L - 1:L, :]
    w_end = jnp.exp(b_last + a_col - m_new)
    decay = jnp.exp(b_last + m_prev - m_new)
    kw = kb.astype(F32) * w_end
    c_s[...] = decay * c_s[...] + lax.dot_general(
        kw.astype(BF16), vb, (((0,), (0,)), ((), ())), preferred_element_type=F32)
    n_s[...] = decay * n_s[...] + jnp.sum(kw, axis=0, keepdims=True)
    m_s[...] = jnp.broadcast_to(m_new, m_s.shape)

    @pl.when(ci == pl.num_programs(2) - 1)
    def _():
        c_out[...] = c_s[...]
        n_out[...] = n_s[...]
        m_out[...] = m_s[...]


def _mlstm(q, k, v, o, gates, gate_bias, g_head, nb, nc, chunk, n_valid, init=None):
    L = chunk
    rows = lambda b, h, c: (b * nc + c, h)
    state = lambda b, h, c: (b, h, 0, 0)
    in_specs = [pl.BlockSpec((L, DK), rows), pl.BlockSpec((L, DK), rows),
                pl.BlockSpec((L, DV), rows), pl.BlockSpec((L, DV), rows),
                pl.BlockSpec((L, LANES), lambda b, h, c: (b * nc + c, 0)),
                pl.BlockSpec((1, LANES), lambda b, h, c: (0, 0)),
                pl.BlockSpec((1, DV), lambda b, h, c: (0, h))]
    args = [q, k, v, o, gates, gate_bias, g_head]
    if init is not None:
        in_specs += [pl.BlockSpec((None, None, DK, DV), state),
                     pl.BlockSpec((None, None, 1, DK), state),
                     pl.BlockSpec((None, None, 1, LANES), state)]
        args += list(init)
    kern = functools.partial(_mlstm_kernel, chunk=L, n_valid=n_valid, has_init=init is not None)
    return pl.pallas_call(
        kern,
        grid=(nb, N_HEADS, nc),
        in_specs=in_specs,
        out_specs=[pl.BlockSpec((L, DV), rows),
                   pl.BlockSpec((None, None, DK, DV), state),
                   pl.BlockSpec((None, None, 1, DK), state),
                   pl.BlockSpec((None, None, 1, LANES), state)],
        out_shape=[jax.ShapeDtypeStruct((nb * nc * L, N_HEADS * DV), BF16),
                   jax.ShapeDtypeStruct((nb, N_HEADS, DK, DV), F32),
                   jax.ShapeDtypeStruct((nb, N_HEADS, 1, DK), F32),
                   jax.ShapeDtypeStruct((nb, N_HEADS, 1, LANES), F32)],
        scratch_shapes=[pltpu.VMEM((DK, DV), F32), pltpu.VMEM((1, DK), F32),
                        pltpu.VMEM((1, LANES), F32)],
        compiler_params=_params("arbitrary", "arbitrary", "arbitrary"),
        name="mlstm",
    )(*args)


def _gmlp_kernel(u_ref, v_ref, gln_ref, bln_ref, wsp_ref, bsp_ref, coef_ref, bias_ref,
                 out_ref, vg_ref, *, n_prompt_tiles, sample_len):
    x = v_ref[...]
    mu = jnp.mean(x, axis=-1, keepdims=True)
    xc = x - mu
    var = jnp.mean(xc * xc, axis=-1, keepdims=True)
    vg = xc * lax.rsqrt(var + EPS) * gln_ref[...] + bln_ref[...]
    u = u_ref[...]
    rows, width = x.shape
    gw = width // N_GROUPS
    is_prompt = pl.program_id(0) < n_prompt_tiles

    @pl.when(is_prompt)
    def _():
        r_i = lax.broadcasted_iota(jnp.int32, (rows, rows), 0)
        c_i = lax.broadcasted_iota(jnp.int32, (rows, rows), 1)
        for g in range(N_GROUPS):
            w = jnp.where(r_i >= c_i, wsp_ref[g], 0.0).astype(BF16)
            sl = slice(g * gw, (g + 1) * gw)
            mixed = jnp.dot(w, vg[:, sl].astype(BF16), preferred_element_type=F32) + bsp_ref[:, g:g + 1]
            out_ref[:, sl] = (u[:, sl] * mixed).astype(out_ref.dtype)

    @pl.when(jnp.logical_not(is_prompt))
    def _():
        vg_ref[...] = vg
        for g in range(N_GROUPS):
            sl = slice(g * gw, (g + 1) * gw)
            vgg = vg[:, sl]
            acc = bias_ref[:, g:g + 1] + coef_ref[:, g * sample_len:g * sample_len + 1] * vgg
            for j in range(1, sample_len):
                cj = coef_ref[:, g * sample_len + j:g * sample_len + j + 1]
                acc = acc + cj * pltpu.roll(vgg, j, axis=0)
            out_ref[:, sl] = (u[:, sl] * acc).astype(out_ref.dtype)


def _gmlp(tail, g_ln, b_ln, w_sp, b_sp_t, coef, bias, n_prompt_rows, n_sample_rows, sample_len):
    width = g_ln.shape[1]
    tr = GMLP_CHUNK
    n_pt = n_prompt_rows // tr
    n_st = n_sample_rows // tr
    kern = functools.partial(_gmlp_kernel, n_prompt_tiles=n_pt, sample_len=sample_len)
    const2 = lambda i: (0, 0)
    return pl.pallas_call(
        kern,
        grid=(n_pt + n_st,),
        in_specs=[pl.BlockSpec((tr, width), lambda i: (i, 0)),
                  pl.BlockSpec((tr, width), lambda i: (i, 1)),
                  pl.BlockSpec((1, width), const2), pl.BlockSpec((1, width), const2),
                  pl.BlockSpec((N_GROUPS, tr, tr), lambda i: (0, 0, 0)),
                  pl.BlockSpec((tr, N_GROUPS), const2),
                  pl.BlockSpec((tr, N_GROUPS * sample_len), const2),
                  pl.BlockSpec((tr, N_GROUPS), const2)],
        out_specs=[pl.BlockSpec((tr, width), lambda i: (i, 0)),
                   pl.BlockSpec((tr, width), lambda i: (jnp.maximum(i - n_pt, 0), 0))],
        out_shape=[jax.ShapeDtypeStruct((n_prompt_rows + n_sample_rows, width), BF16),
                   jax.ShapeDtypeStruct((n_sample_rows, width), F32)],
        compiler_params=_params("arbitrary"),
        name="gmlp",
    )(tail, tail, g_ln, b_ln, w_sp, b_sp_t, coef, bias)


def kernel(x_prompt, x_sample, p_prompt, p_sample, state_mlstm_conv, state_mlstm_C, state_mlstm_n, state_mlstm_m, g_ffn1_pre, w_ffn1_gate, w_ffn1_up, w_ffn1_down, g_ffn1_post, g_mix_pre, w_in, w_conv, b_conv, b_igate, b_fgate, g_head, w_a_out, g_ln_v, b_ln_v, w_spatial, b_spatial, w_b_out, w_o, g_mix_post, g_ffn2_pre, w_ffn2_gate, w_ffn2_up, w_ffn2_down, g_ffn2_post, g_ple_pre, w_ple_gate, w_ple_up, g_ple_post):
    assert w_in.shape[0] == 1, "single layer"
    nbp, seq, d = x_prompt.shape
    nbs, sseq, _ = x_sample.shape
    mp, ms = nbp * seq, nbs * sseq
    qk_w = 2 * N_HEADS * DK
    v_w = N_HEADS * DV
    d_b = g_ln_v.shape[1]
    tail_off = qk_w + 2 * v_w + 2 * N_HEADS

    x_all = jnp.concatenate([x_prompt.reshape(mp, d), x_sample.reshape(ms, d)], axis=0)
    p_all = jnp.concatenate([p_prompt[0].reshape(mp, -1), p_sample[0].reshape(ms, -1)], axis=0)

    xn1 = _rms_cast(x_all, g_ffn1_pre)
    hid1 = _ffn_up(xn1, w_ffn1_gate[0], w_ffn1_up[0])
    h1, xn2 = _ffn_down(hid1, w_ffn1_down[0].astype(BF16), x_all, g_ffn1_post, g_mix_pre)

    w_in0 = w_in[0]
    qk_pre = _mm(xn2, w_in0, 0, qk_w, [_identity], F32)
    v_a = _mm(xn2, w_in0, qk_w, v_w, [_identity], BF16)
    o_sig = _mm(xn2, w_in0, qk_w + v_w, v_w, [_sigmoid], F32)
    w_gates = jnp.pad(w_in0[:, qk_w + 2 * v_w:tail_off], ((0, 0), (0, LANES - 2 * N_HEADS)))
    gates = _mm(xn2, w_gates, 0, LANES, [_identity], F32)
    tail = _mm(xn2, w_in0[:, tail_off:], 0, 2 * d_b + 2 * d, [_gelu, _sigmoid], F32)
    gate_bias = jnp.pad(jnp.concatenate([b_igate[0], b_fgate[0]]), (0, LANES - 2 * N_HEADS)).reshape(1, LANES)

    conv_w, conv_b = w_conv[0], b_conv
    q_p, k_p = _conv(qk_pre, jnp.zeros((nbp * SUBLANES, qk_w), F32), conv_w, conv_b, nbp, seq, 256)
    h_p, c_p, n_p, m_p = _mlstm(q_p, k_p, v_a, o_sig, gates, gate_bias, g_head,
                                nbp, seq // MLSTM_CHUNK, MLSTM_CHUNK, MLSTM_CHUNK)

    def pad_sample(a):
        a = a[mp:].reshape(nbs, sseq, -1)
        return jnp.pad(a, ((0, 0), (0, SAMPLE_PAD - sseq), (0, 0))).reshape(nbs * SAMPLE_PAD, -1)

    qk_s = qk_pre[mp:].reshape(nbs, sseq, qk_w)
    p0_s = jnp.concatenate([jnp.zeros((nbs, SUBLANES - (CONV_W - 1), qk_w), F32), state_mlstm_conv[0]],
                           axis=1).reshape(nbs * SUBLANES, qk_w)
    q_s, k_s = _conv(pad_sample(qk_pre), p0_s, conv_w, conv_b, nbs, SAMPLE_PAD, SAMPLE_PAD)
    init = (state_mlstm_C[0], state_mlstm_n[0].reshape(nbs, N_HEADS, 1, DK),
            jnp.broadcast_to(state_mlstm_m[0][:, :, None, None], (nbs, N_HEADS, 1, LANES)))
    h_s, c_s, n_s, m_s = _mlstm(q_s, k_s, pad_sample(v_a), pad_sample(o_sig), pad_sample(gates),
                                gate_bias, g_head, nbs, 1, SAMPLE_PAD, sseq, init=init)
    h_s = h_s.reshape(nbs, SAMPLE_PAD, v_w)[:, :sseq].reshape(ms, v_w)
    ha_all = jnp.concatenate([h_p, h_s], axis=0)

    t_idx = jnp.arange(GMLP_CHUNK) % sseq
    shifts = jnp.arange(sseq)
    src = t_idx[:, None] - shifts[None, :]
    w_small = w_spatial[0][:, :sseq, :sseq]
    coef = jnp.where(src[None] >= 0, w_small[:, t_idx[:, None], jnp.maximum(src, 0)], 0.0)
    coef = coef.transpose(1, 0, 2).reshape(GMLP_CHUNK, N_GROUPS * sseq)
    bias_s = b_spatial[0][:, t_idx].T
    sg_all, vg_s = _gmlp(tail, g_ln_v, b_ln_v, w_spatial[0], b_spatial[0].T, coef, bias_s, mp, ms, sseq)

    h2, xn3 = _merge(ha_all, sg_all, tail, h1, w_a_out[0].astype(BF16), w_b_out[0].astype(BF16),
                     w_o[0].astype(BF16), g_mix_post, g_ffn2_pre)
    hid2 = _ffn_up(xn3, w_ffn2_gate[0], w_ffn2_up[0])
    h3, xn4 = _ffn_down(hid2, w_ffn2_down[0].astype(BF16), h2, g_ffn2_post, g_ple_pre)
    out = _ple(xn4, p_all, h3, w_ple_gate[0].astype(BF16), w_ple_up[0].astype(BF16), g_ple_post)

    y_prompt = out[:mp].reshape(nbp, seq, d)
    y_sample = out[mp:].reshape(nbs, sseq, d)
    conv_prompt = qk_pre[:mp].reshape(nbp, seq, qk_w)[:, seq - (CONV_W - 1):][None]
    conv_sample = qk_s[:, sseq - (CONV_W - 1):][None]
    return (y_prompt, y_sample,
            conv_prompt, c_p[None], n_p.reshape(1, nbp, N_HEADS, DK), m_p[:, :, 0, 0][None],
            conv_sample, c_s[None], n_s.reshape(1, nbs, N_HEADS, DK), m_s[:, :, 0, 0][None],
            vg_s.reshape(1, nbs, sseq, d_b))
```

```python
import functools

import jax
import jax.numpy as jnp
from jax import lax
from jax.experimental import pallas as pl
from jax.experimental.pallas import tpu as pltpu

F32 = jnp.float32
BF16 = jnp.bfloat16
EPS = 1e-6

N_HEADS = 4
DK = 256
DV = 512
CONV_W = 4
N_GROUPS = 4
GMLP_CHUNK = 128
LANES = 128
SUBLANES = 8
MLSTM_CHUNK = 256
SAMPLE_PAD = 16
VMEM_LIMIT = 54 * 1024 * 1024


def _params(*sem):
    return pltpu.CompilerParams(dimension_semantics=sem, vmem_limit_bytes=VMEM_LIMIT)


def _rms(x, g):
    return x * lax.rsqrt(jnp.mean(x * x, axis=-1, keepdims=True) + EPS) * g


def _gelu(x):
    return 0.5 * x * (1.0 + lax.erf(x * 0.7071067811865476))


def _sigmoid(x):
    return jax.nn.sigmoid(x)


def _identity(x):
    return x


def _resident(shape):
    return pl.BlockSpec(shape, lambda *_: (0,) * len(shape), pipeline_mode=pl.Buffered(1))


def _rms_cast_kernel(x_ref, g_ref, o_ref):
    o_ref[...] = _rms(x_ref[...], g_ref[...]).astype(o_ref.dtype)


def _rms_cast(x, g, tm=512):
    m, d = x.shape
    return pl.pallas_call(
        _rms_cast_kernel,
        grid=(m // tm,),
        in_specs=[pl.BlockSpec((tm, d), lambda i: (i, 0)),
                  pl.BlockSpec((1, d), lambda i: (0, 0))],
        out_specs=pl.BlockSpec((tm, d), lambda i: (i, 0)),
        out_shape=jax.ShapeDtypeStruct((m, d), BF16),
        compiler_params=_params("arbitrary"),
        name="rms_cast",
    )(x, g)


def _ffn_up_kernel(x_ref, wg_ref, wu_ref, o_ref, wgb, wub):
    @pl.when(pl.program_id(1) == 0)
    def _():
        wgb[...] = wg_ref[...].astype(BF16)
        wub[...] = wu_ref[...].astype(BF16)

    x = x_ref[...]
    g = jnp.dot(x, wgb[...], preferred_element_type=F32)
    u = jnp.dot(x, wub[...], preferred_element_type=F32)
    o_ref[...] = (g * _sigmoid(g) * u).astype(o_ref.dtype)


def _ffn_up(xn, wg, wu, tm=1088, tf=512):
    m, d = xn.shape
    f = wg.shape[1]
    return pl.pallas_call(
        _ffn_up_kernel,
        grid=(f // tf, m // tm),
        in_specs=[pl.BlockSpec((tm, d), lambda j, i: (i, 0)),
                  pl.BlockSpec((d, tf), lambda j, i: (0, j)),
                  pl.BlockSpec((d, tf), lambda j, i: (0, j))],
        out_specs=pl.BlockSpec((tm, tf), lambda j, i: (i, j)),
        out_shape=jax.ShapeDtypeStruct((m, f), BF16),
        scratch_shapes=[pltpu.VMEM((d, tf), BF16), pltpu.VMEM((d, tf), BF16)],
        compiler_params=_params("arbitrary", "arbitrary"),
        name="ffn_up",
    )(xn, wg, wu)


def _mm_kernel(x_ref, w_ref, o_ref, wb, *, act):
    @pl.when(pl.program_id(1) == 0)
    def _():
        wb[...] = w_ref[...].astype(BF16)

    y = jnp.dot(x_ref[...], wb[...], preferred_element_type=F32)
    o_ref[...] = act(y).astype(o_ref.dtype)


def _mm(x, w, col_off, n_cols, act, out_dtype, tm=1088, tn=1024):
    m, k = x.shape
    tn = min(tn, n_cols)
    n_tiles = n_cols // tn
    off = col_off // tn
    assert off * tn == col_off and n_tiles * tn == n_cols
    kern = functools.partial(_mm_kernel, act=act)
    return pl.pallas_call(
        kern,
        grid=(n_tiles, m // tm),
        in_specs=[pl.BlockSpec((tm, k), lambda j, i: (i, 0)),
                  pl.BlockSpec((k, tn), lambda j, i: (0, off + j))],
        out_specs=pl.BlockSpec((tm, tn), lambda j, i: (i, j)),
        out_shape=jax.ShapeDtypeStruct((m, n_cols), out_dtype),
        scratch_shapes=[pltpu.VMEM((k, tn), BF16)],
        compiler_params=_params("arbitrary", "arbitrary"),
        name="mm",
    )(x, w)


def _ffn_down_kernel(h_ref, wd_ref, x_ref, gpost_ref, gnext_ref, hout_ref, xn_ref):
    y = jnp.dot(h_ref[...], wd_ref[...], preferred_element_type=F32)
    h = x_ref[...] + 0.5 * _rms(y, gpost_ref[...])
    hout_ref[...] = h
    xn_ref[...] = _rms(h, gnext_ref[...]).astype(xn_ref.dtype)


def _ffn_down(hid, wd, x, g_post, g_next, tm=256):
    m, f = hid.shape
    d = wd.shape[1]
    row = lambda i: (i, 0)
    return pl.pallas_call(
        _ffn_down_kernel,
        grid=(m // tm,),
        in_specs=[pl.BlockSpec((tm, f), row), _resident((f, d)),
                  pl.BlockSpec((tm, d), row), _resident((1, d)), _resident((1, d))],
        out_specs=[pl.BlockSpec((tm, d), row), pl.BlockSpec((tm, d), row)],
        out_shape=[jax.ShapeDtypeStruct((m, d), F32), jax.ShapeDtypeStruct((m, d), BF16)],
        compiler_params=_params("arbitrary"),
        name="ffn_down",
    )(hid, wd, x, g_post, g_next)


def _merge_kernel(ha_ref, sg_ref, gates_a_ref, gates_b_ref, h_ref, wa_ref, wb_ref, wo_ref,
                  gpost_ref, gnext_ref, hout_ref, xn_ref):
    ya = jnp.dot(ha_ref[...], wa_ref[...], preferred_element_type=F32)
    yb = jnp.dot(sg_ref[...], wb_ref[...], preferred_element_type=F32)
    mixin = (gates_a_ref[...] * ya + gates_b_ref[...] * yb).astype(BF16)
    mix = jnp.dot(mixin, wo_ref[...], preferred_element_type=F32)
    h = h_ref[...] + _rms(mix, gpost_ref[...])
    hout_ref[...] = h
    xn_ref[...] = _rms(h, gnext_ref[...]).astype(xn_ref.dtype)


def _merge(ha, sg, ab_sig, h, wa, wb, wo, g_post, g_next, tm=256):
    m, d = h.shape
    row = lambda i: (i, 0)
    return pl.pallas_call(
        _merge_kernel,
        grid=(m // tm,),
        in_specs=[pl.BlockSpec((tm, d), row), pl.BlockSpec((tm, d), row),
                  pl.BlockSpec((tm, d), lambda i: (i, 0)), pl.BlockSpec((tm, d), lambda i: (i, 1)),
                  pl.BlockSpec((tm, d), row),
                  _resident((d, d)), _resident((d, d)), _resident((d, d)),
                  _resident((1, d)), _resident((1, d))],
        out_specs=[pl.BlockSpec((tm, d), row), pl.BlockSpec((tm, d), row)],
        out_shape=[jax.ShapeDtypeStruct((m, d), F32), jax.ShapeDtypeStruct((m, d), BF16)],
        compiler_params=_params("arbitrary"),
        name="merge",
    )(ha, sg, ab_sig, ab_sig, h, wa, wb, wo, g_post, g_next)


def _ple_kernel(xn_ref, p_ref, h_ref, wg_ref, wu_ref, gpost_ref, out_ref):
    gate = _sigmoid(jnp.dot(xn_ref[...], wg_ref[...], preferred_element_type=F32))
    up = jnp.dot(p_ref[...].astype(BF16), wu_ref[...], preferred_element_type=F32)
    out_ref[...] = h_ref[...] + _rms(gate * up, gpost_ref[...])


def _ple(xn, p, h, wg, wu, g_post, tm=256):
    m, d = h.shape
    dp = p.shape[1]
    row = lambda i: (i, 0)
    return pl.pallas_call(
        _ple_kernel,
        grid=(m // tm,),
        in_specs=[pl.BlockSpec((tm, d), row), pl.BlockSpec((tm, dp), row), pl.BlockSpec((tm, d), row),
                  _resident((d, d)), _resident((dp, d)), _resident((1, d))],
        out_specs=pl.BlockSpec((tm, d), row),
        out_shape=jax.ShapeDtypeStruct((m, d), F32),
        compiler_params=_params("arbitrary"),
        name="ple",
    )(xn, p, h, wg, wu, g_post)


def _conv_kernel(x_ref, p0_ref, w_ref, b_ref, q_ref, k_ref, prev, *, ts):
    @pl.when(pl.program_id(1) == 0)
    def _():
        prev[...] = p0_ref[...]

    x = x_ref[...]
    tail = prev[...]
    w = w_ref[...]
    c = x.shape[1]
    y = b_ref[...] + x * w[CONV_W - 1:CONV_W, :]
    rowid = lax.broadcasted_iota(jnp.int32, (SUBLANES, c), 0)
    for j in range(1, CONV_W):
        xr = pltpu.roll(x, j, axis=0)
        tr = pltpu.roll(tail, j, axis=0)
        top = jnp.where(rowid < j, tr, xr[0:SUBLANES])
        xs = jnp.concatenate([top, xr[SUBLANES:]], axis=0)
        y = y + xs * w[CONV_W - 1 - j:CONV_W - j, :]
    prev[...] = x[ts - SUBLANES:ts]
    s = y * _sigmoid(y)
    half = c // 2
    q_ref[...] = s[:, :half].astype(q_ref.dtype)
    k_ref[...] = (s[:, half:] * (DK ** -0.5)).astype(k_ref.dtype)


def _conv(x, p0, w, b, nb, seq, ts):
    c = x.shape[1]
    nt = seq // ts
    kern = functools.partial(_conv_kernel, ts=ts)
    return pl.pallas_call(
        kern,
        grid=(nb, nt),
        in_specs=[pl.BlockSpec((ts, c), lambda b_, t: (b_ * nt + t, 0)),
                  pl.BlockSpec((SUBLANES, c), lambda b_, t: (b_, 0)),
                  pl.BlockSpec((CONV_W, c), lambda b_, t: (0, 0)),
                  pl.BlockSpec((1, c), lambda b_, t: (0, 0))],
        out_specs=[pl.BlockSpec((ts, c // 2), lambda b_, t: (b_ * nt + t, 0)),
                   pl.BlockSpec((ts, c // 2), lambda b_, t: (b_ * nt + t, 0))],
        out_shape=[jax.ShapeDtypeStruct((nb * seq, c // 2), BF16),
                   jax.ShapeDtypeStruct((nb * seq, c // 2), BF16)],
        scratch_shapes=[pltpu.VMEM((SUBLANES, c), F32)],
        compiler_params=_params("arbitrary", "arbitrary"),
        name="conv",
    )(x, p0, w, b)


def _mlstm_kernel(*refs, chunk, n_valid, has_init):
    if has_init:
        (q_ref, k_ref, v_ref, o_ref, g_ref, gb_ref, gh_ref, c0_ref, n0_ref, m0_ref,
         h_out, c_out, n_out, m_out, c_s, n_s, m_s) = refs
    else:
        (q_ref, k_ref, v_ref, o_ref, g_ref, gb_ref, gh_ref,
         h_out, c_out, n_out, m_out, c_s, n_s, m_s) = refs
    L = chunk
    head = pl.program_id(1)
    ci = pl.program_id(2)

    @pl.when(ci == 0)
    def _():
        if has_init:
            c_s[...] = c0_ref[...]
            n_s[...] = n0_ref[...]
            m_s[...] = m0_ref[...]
        else:
            c_s[...] = jnp.zeros_like(c_s)
            n_s[...] = jnp.zeros_like(n_s)
            m_s[...] = jnp.zeros_like(m_s)

    g = g_ref[...] + gb_ref[...]
    lane = lax.broadcasted_iota(jnp.int32, g.shape, 1)
    ig = jnp.sum(jnp.where(lane == head, g, 0.0), axis=1, keepdims=True)
    fg = jnp.sum(jnp.where(lane == head + N_HEADS, g, 0.0), axis=1, keepdims=True)
    lf = jnp.minimum(fg, 0.0) - jnp.log1p(jnp.exp(-jnp.abs(fg)))
    if n_valid < L:
        rcol = lax.broadcasted_iota(jnp.int32, (L, 1), 0)
        ig = jnp.where(rcol < n_valid, ig, -jnp.inf)
        lf = jnp.where(rcol < n_valid, lf, 0.0)

    bc = jnp.broadcast_to(lf, (L, LANES))
    rowi = lax.broadcasted_iota(jnp.int32, (L, LANES), 0)
    sh = 1
    while sh < L:
        bc = bc + jnp.where(rowi >= sh, pltpu.roll(bc, sh, axis=0), 0.0)
        sh *= 2
    bcum = bc[:, 0:1]

    m_prev = m_s[0:1, 0:1]
    m_in = bcum + m_prev
    a_col = ig - bcum
    r_i = lax.broadcasted_iota(jnp.int32, (L, L), 0)
    c_i = lax.broadcasted_iota(jnp.int32, (L, L), 1)
    a_row = jnp.sum(jnp.where(r_i == c_i, jnp.broadcast_to(a_col, (L, L)), 0.0),
                    axis=0, keepdims=True)
    d = jnp.where(r_i >= c_i, bcum + a_row, -jnp.inf)
    m_t = jnp.maximum(m_in, jnp.max(d, axis=1, keepdims=True))
    p = jnp.exp(d - m_t)

    qb = q_ref[...]
    kb = k_ref[...]
    vb = v_ref[...]
    s = lax.dot_general(qb, kb, (((1,), (1,)), ((), ())), preferred_element_type=F32) * p
    w_prev = jnp.exp(m_in - m_t)
    cb = c_s[...].astype(BF16)
    num = (jnp.dot(s.astype(BF16), vb, preferred_element_type=F32)
           + w_prev * jnp.dot(qb, cb, preferred_element_type=F32))
    den = (jnp.sum(s, axis=1, keepdims=True)
           + w_prev * jnp.sum(qb.astype(F32) * n_s[...], axis=1, keepdims=True))
    hh = num / jnp.maximum(jnp.abs(den), jnp.exp(-m_t))

    hn = hh * lax.rsqrt(jnp.mean(hh * hh, axis=1, keepdims=True) + EPS) * gh_ref[...]
    h_out[...] = (o_ref[...] * hn).astype(h_out.dtype)

    m_new = m_t[L - 1:L, :]
    b_last = bcum[L - 1:L, :]
    w_end = jnp.exp(b_last + a_col - m_new)
    decay = jnp.exp(b_last + m_prev - m_new)
    kw = kb.astype(F32) * w_end
    c_s[...] = decay * c_s[...] + lax.dot_general(
        kw.astype(BF16), vb, (((0,), (0,)), ((), ())), preferred_element_type=F32)
    n_s[...] = decay * n_s[...] + jnp.sum(kw, axis=0, keepdims=True)
    m_s[...] = jnp.broadcast_to(m_new, m_s.shape)

    @pl.when(ci == pl.num_programs(2) - 1)
    def _():
        c_out[...] = c_s[...]
        n_out[...] = n_s[...]
        m_out[...] = m_s[...]


def _mlstm(q, k, v, o, gates, gate_bias, g_head, nb, nc, chunk, n_valid, init=None):
    L = chunk
    rows = lambda b, h, c: (b * nc + c, h)
    state = lambda b, h, c: (b, h, 0, 0)
    in_specs = [pl.BlockSpec((L, DK), rows), pl.BlockSpec((L, DK), rows),
                pl.BlockSpec((L, DV), rows), pl.BlockSpec((L, DV), rows),
                pl.BlockSpec((L, LANES), lambda b, h, c: (b * nc + c, 0)),
                pl.BlockSpec((1, LANES), lambda b, h, c: (0, 0)),
                pl.BlockSpec((1, DV), lambda b, h, c: (0, h))]
    args = [q, k, v, o, gates, gate_bias, g_head]
    if init is not None:
        in_specs += [pl.BlockSpec((None, None, DK, DV), state),
                     pl.BlockSpec((None, None, 1, DK), state),
                     pl.BlockSpec((None, None, 1, LANES), state)]
        args += list(init)
    kern = functools.partial(_mlstm_kernel, chunk=L, n_valid=n_valid, has_init=init is not None)
    return pl.pallas_call(
        kern,
        grid=(nb, N_HEADS, nc),
        in_specs=in_specs,
        out_specs=[pl.BlockSpec((L, DV), rows),
                   pl.BlockSpec((None, None, DK, DV), state),
                   pl.BlockSpec((None, None, 1, DK), state),
                   pl.BlockSpec((None, None, 1, LANES), state)],
        out_shape=[jax.ShapeDtypeStruct((nb * nc * L, N_HEADS * DV), BF16),
                   jax.ShapeDtypeStruct((nb, N_HEADS, DK, DV), F32),
                   jax.ShapeDtypeStruct((nb, N_HEADS, 1, DK), F32),
                   jax.ShapeDtypeStruct((nb, N_HEADS, 1, LANES), F32)],
        scratch_shapes=[pltpu.VMEM((DK, DV), F32), pltpu.VMEM((1, DK), F32),
                        pltpu.VMEM((1, LANES), F32)],
        compiler_params=_params("arbitrary", "arbitrary", "arbitrary"),
        name="mlstm",
    )(*args)


def _gmlp_kernel(u_ref, v_ref, gln_ref, bln_ref, wsp_ref, bsp_ref, coef_ref, bias_ref,
                 out_ref, vg_ref, *, n_prompt_tiles, sample_len):
    x = v_ref[...]
    mu = jnp.mean(x, axis=-1, keepdims=True)
    xc = x - mu
    var = jnp.mean(xc * xc, axis=-1, keepdims=True)
    vg = xc * lax.rsqrt(var + EPS) * gln_ref[...] + bln_ref[...]
    u = u_ref[...]
    rows, width = x.shape
    gw = width // N_GROUPS
    is_prompt = pl.program_id(0) < n_prompt_tiles

    @pl.when(is_prompt)
    def _():
        r_i = lax.broadcasted_iota(jnp.int32, (rows, rows), 0)
        c_i = lax.broadcasted_iota(jnp.int32, (rows, rows), 1)
        for g in range(N_GROUPS):
            w = jnp.where(r_i >= c_i, wsp_ref[g], 0.0).astype(BF16)
            sl = slice(g * gw, (g + 1) * gw)
            mixed = jnp.dot(w, vg[:, sl].astype(BF16), preferred_element_type=F32) + bsp_ref[:, g:g + 1]
            out_ref[:, sl] = (u[:, sl] * mixed).astype(out_ref.dtype)

    @pl.when(jnp.logical_not(is_prompt))
    def _():
        vg_ref[...] = vg
        for g in range(N_GROUPS):
            sl = slice(g * gw, (g + 1) * gw)
            vgg = vg[:, sl]
            acc = bias_ref[:, g:g + 1] + coef_ref[:, g * sample_len:g * sample_len + 1] * vgg
            for j in range(1, sample_len):
                cj = coef_ref[:, g * sample_len + j:g * sample_len + j + 1]
                acc = acc + cj * pltpu.roll(vgg, j, axis=0)
            out_ref[:, sl] = (u[:, sl] * acc).astype(out_ref.dtype)


def _gmlp(uv, g_ln, b_ln, w_sp, b_sp_t, coef, bias, n_prompt_rows, n_sample_rows, sample_len):
    width = g_ln.shape[1]
    tr = GMLP_CHUNK
    n_pt = n_prompt_rows // tr
    n_st = n_sample_rows // tr
    kern = functools.partial(_gmlp_kernel, n_prompt_tiles=n_pt, sample_len=sample_len)
    const2 = lambda i: (0, 0)
    return pl.pallas_call(
        kern,
        grid=(n_pt + n_st,),
        in_specs=[pl.BlockSpec((tr, width), lambda i: (i, 0)),
                  pl.BlockSpec((tr, width), lambda i: (i, 1)),
                  pl.BlockSpec((1, width), const2), pl.BlockSpec((1, width), const2),
                  pl.BlockSpec((N_GROUPS, tr, tr), lambda i: (0, 0, 0)),
                  pl.BlockSpec((tr, N_GROUPS), const2),
                  pl.BlockSpec((tr, N_GROUPS * sample_len), const2),
                  pl.BlockSpec((tr, N_GROUPS), const2)],
        out_specs=[pl.BlockSpec((tr, width), lambda i: (i, 0)),
                   pl.BlockSpec((tr, width), lambda i: (jnp.maximum(i - n_pt, 0), 0))],
        out_shape=[jax.ShapeDtypeStruct((n_prompt_rows + n_sample_rows, width), BF16),
                   jax.ShapeDtypeStruct((n_sample_rows, width), F32)],
        compiler_params=_params("arbitrary"),
        name="gmlp",
    )(uv, uv, g_ln, b_ln, w_sp, b_sp_t, coef, bias)


def kernel(x_prompt, x_sample, p_prompt, p_sample, state_mlstm_conv, state_mlstm_C, state_mlstm_n, state_mlstm_m, g_ffn1_pre, w_ffn1_gate, w_ffn1_up, w_ffn1_down, g_ffn1_post, g_mix_pre, w_in, w_conv, b_conv, b_igate, b_fgate, g_head, w_a_out, g_ln_v, b_ln_v, w_spatial, b_spatial, w_b_out, w_o, g_mix_post, g_ffn2_pre, w_ffn2_gate, w_ffn2_up, w_ffn2_down, g_ffn2_post, g_ple_pre, w_ple_gate, w_ple_up, g_ple_post):
    assert w_in.shape[0] == 1, "single layer"
    nbp, seq, d = x_prompt.shape
    nbs, sseq, _ = x_sample.shape
    mp, ms = nbp * seq, nbs * sseq
    qk_w = 2 * N_HEADS * DK
    v_w = N_HEADS * DV
    d_b = g_ln_v.shape[1]
    tail_off = qk_w + 2 * v_w + 2 * N_HEADS

    x_all = jnp.concatenate([x_prompt.reshape(mp, d), x_sample.reshape(ms, d)], axis=0)
    p_all = jnp.concatenate([p_prompt[0].reshape(mp, -1), p_sample[0].reshape(ms, -1)], axis=0)

    xn1 = _rms_cast(x_all, g_ffn1_pre)
    hid1 = _ffn_up(xn1, w_ffn1_gate[0], w_ffn1_up[0])
    h1, xn2 = _ffn_down(hid1, w_ffn1_down[0].astype(BF16), x_all, g_ffn1_post, g_mix_pre)

    w_in0 = w_in[0]
    qk_pre = _mm(xn2, w_in0, 0, qk_w, _identity, F32)
    v_a = _mm(xn2, w_in0, qk_w, v_w, _identity, BF16)
    o_sig = _mm(xn2, w_in0, qk_w + v_w, v_w, _sigmoid, F32)
    w_gates = jnp.pad(w_in0[:, qk_w + 2 * v_w:tail_off], ((0, 0), (0, LANES - 2 * N_HEADS)))
    gates = _mm(xn2, w_gates, 0, LANES, _identity, F32)
    w_tail = w_in0[:, tail_off:]
    uv_gelu = _mm(xn2, w_tail, 0, 2 * d_b, _gelu, F32)
    ab_sig = _mm(xn2, w_tail, 2 * d_b, 2 * d, _sigmoid, F32)
    gate_bias = jnp.pad(jnp.concatenate([b_igate[0], b_fgate[0]]), (0, LANES - 2 * N_HEADS)).reshape(1, LANES)

    conv_w, conv_b = w_conv[0], b_conv
    q_p, k_p = _conv(qk_pre, jnp.zeros((nbp * SUBLANES, qk_w), F32), conv_w, conv_b, nbp, seq, 256)
    h_p, c_p, n_p, m_p = _mlstm(q_p, k_p, v_a, o_sig, gates, gate_bias, g_head,
                                nbp, seq // MLSTM_CHUNK, MLSTM_CHUNK, MLSTM_CHUNK)

    def pad_sample(a):
        a = a[mp:].reshape(nbs, sseq, -1)
        return jnp.pad(a, ((0, 0), (0, SAMPLE_PAD - sseq), (0, 0))).reshape(nbs * SAMPLE_PAD, -1)

    qk_s = qk_pre[mp:].reshape(nbs, sseq, qk_w)
    p0_s = jnp.concatenate([jnp.zeros((nbs, SUBLANES - (CONV_W - 1), qk_w), F32), state_mlstm_conv[0]],
                           axis=1).reshape(nbs * SUBLANES, qk_w)
    q_s, k_s = _conv(pad_sample(qk_pre), p0_s, conv_w, conv_b, nbs, SAMPLE_PAD, SAMPLE_PAD)
    init = (state_mlstm_C[0], state_mlstm_n[0].reshape(nbs, N_HEADS, 1, DK),
            jnp.broadcast_to(state_mlstm_m[0][:, :, None, None], (nbs, N_HEADS, 1, LANES)))
    h_s, c_s, n_s, m_s = _mlstm(q_s, k_s, pad_sample(v_a), pad_sample(o_sig), pad_sample(gates),
                                gate_bias, g_head, nbs, 1, SAMPLE_PAD, sseq, init=init)
    h_s = h_s.reshape(nbs, SAMPLE_PAD, v_w)[:, :sseq].reshape(ms, v_w)
    ha_all = jnp.concatenate([h_p, h_s], axis=0)

    t_idx = jnp.arange(GMLP_CHUNK) % sseq
    shifts = jnp.arange(sseq)
    src = t_idx[:, None] - shifts[None, :]
    w_small = w_spatial[0][:, :sseq, :sseq]
    coef = jnp.where(src[None] >= 0, w_small[:, t_idx[:, None], jnp.maximum(src, 0)], 0.0)
    coef = coef.transpose(1, 0, 2).reshape(GMLP_CHUNK, N_GROUPS * sseq)
    bias_s = b_spatial[0][:, t_idx].T
    sg_all, vg_s = _gmlp(uv_gelu, g_ln_v, b_ln_v, w_spatial[0], b_spatial[0].T, coef, bias_s, mp, ms, sseq)

    h2, xn3 = _merge(ha_all, sg_all, ab_sig, h1, w_a_out[0].astype(BF16), w_b_out[0].astype(BF16),
                     w_o[0].astype(BF16), g_mix_post, g_ffn2_pre)
    hid2 = _ffn_up(xn3, w_ffn2_gate[0], w_ffn2_up[0])
    h3, xn4 = _ffn_down(hid2, w_ffn2_down[0].astype(BF16), h2, g_ffn2_post, g_ple_pre)
    out = _ple(xn4, p_all, h3, w_ple_gate[0].astype(BF16), w_ple_up[0].astype(BF16), g_ple_post)

    y_prompt = out[:mp].reshape(nbp, seq, d)
    y_sample = out[mp:].reshape(nbs, sseq, d)
    conv_prompt = qk_pre[:mp].reshape(nbp, seq, qk_w)[:, seq - (CONV_W - 1):][None]
    conv_sample = qk_s[:, sseq - (CONV_W - 1):][None]
    return (y_prompt, y_sample,
            conv_prompt, c_p[None], n_p.reshape(1, nbp, N_HEADS, DK), m_p[:, :, 0, 0][None],
            conv_sample, c_s[None], n_s.reshape(1, nbs, N_HEADS, DK), m_s[:, :, 0, 0][None],
            vg_s.reshape(1, nbs, sseq, d_b))
```

```python
import functools

import jax
import jax.numpy as jnp
from jax import lax
from jax.experimental import pallas as pl
from jax.experimental.pallas import tpu as pltpu

F32 = jnp.float32
BF16 = jnp.bfloat16
EPS = 1e-6

N_HEADS = 4
DK = 256
DV = 512
CONV_W = 4
N_GROUPS = 4
GMLP_CHUNK = 128
LANES = 128
SUBLANES = 8
BF16_ROWS = 16
MLSTM_CHUNK = 256
VMEM_LIMIT = 54 * 1024 * 1024


def _params(*sem):
    return pltpu.CompilerParams(dimension_semantics=sem, vmem_limit_bytes=VMEM_LIMIT)


def _rms(x, g):
    return x * lax.rsqrt(jnp.mean(x * x, axis=-1, keepdims=True) + EPS) * g


def _gelu(x):
    return 0.5 * x * (1.0 + lax.erf(x * 0.7071067811865476))


def _sigmoid(x):
    return jax.nn.sigmoid(x)


def _identity(x):
    return x


def _log_sigmoid(x):
    return jnp.minimum(x, 0.0) - jnp.log1p(jnp.exp(-jnp.abs(x)))


def _resident(shape):
    return pl.BlockSpec(shape, lambda *_: (0,) * len(shape), pipeline_mode=pl.Buffered(1))


def _split_specs(tm, width, n_p):
    return [pl.BlockSpec((tm, width), lambda i: (jnp.minimum(i, n_p - 1), 0)),
            pl.BlockSpec((tm, width), lambda i: (jnp.maximum(i - n_p, 0), 0))]


def _head_gates(g, head):
    lane = lax.broadcasted_iota(jnp.int32, g.shape, 1)
    ig = jnp.sum(jnp.where(lane == head, g, 0.0), axis=1, keepdims=True)
    fg = jnp.sum(jnp.where(lane == head + N_HEADS, g, 0.0), axis=1, keepdims=True)
    return ig, fg


def _column_to_row(col, n):
    r_i = lax.broadcasted_iota(jnp.int32, (n, n), 0)
    c_i = lax.broadcasted_iota(jnp.int32, (n, n), 1)
    return jnp.sum(jnp.where(r_i == c_i, jnp.broadcast_to(col, (n, n)), 0.0), axis=0, keepdims=True)


def _rms_cast_kernel(xp_ref, xs_ref, g_ref, o_ref, *, n_p):
    i = pl.program_id(0)

    @pl.when(i < n_p)
    def _():
        o_ref[...] = _rms(xp_ref[...], g_ref[...]).astype(o_ref.dtype)

    @pl.when(i >= n_p)
    def _():
        o_ref[...] = _rms(xs_ref[...], g_ref[...]).astype(o_ref.dtype)


def _rms_cast(x_p, x_s, g, tm=512):
    (mp, d), ms = x_p.shape, x_s.shape[0]
    n_p = mp // tm
    return pl.pallas_call(
        functools.partial(_rms_cast_kernel, n_p=n_p),
        grid=((mp + ms) // tm,),
        in_specs=_split_specs(tm, d, n_p) + [pl.BlockSpec((1, d), lambda i: (0, 0))],
        out_specs=pl.BlockSpec((tm, d), lambda i: (i, 0)),
        out_shape=jax.ShapeDtypeStruct((mp + ms, d), BF16),
        compiler_params=_params("arbitrary"),
        name="rms_cast",
    )(x_p, x_s, g)


def _ffn_up_kernel(x_ref, wg_ref, wu_ref, o_ref, wgb, wub):
    @pl.when(pl.program_id(1) == 0)
    def _():
        wgb[...] = wg_ref[...].astype(BF16)
        wub[...] = wu_ref[...].astype(BF16)

    x = x_ref[...]
    g = jnp.dot(x, wgb[...], preferred_element_type=F32)
    u = jnp.dot(x, wub[...], preferred_element_type=F32)
    o_ref[...] = (g * _sigmoid(g) * u).astype(o_ref.dtype)


def _ffn_up(xn, wg, wu, tm=1088, tf=512):
    m, d = xn.shape
    f = wg.shape[1]
    return pl.pallas_call(
        _ffn_up_kernel,
        grid=(f // tf, m // tm),
        in_specs=[pl.BlockSpec((tm, d), lambda j, i: (i, 0)),
                  pl.BlockSpec((d, tf), lambda j, i: (0, j)),
                  pl.BlockSpec((d, tf), lambda j, i: (0, j))],
        out_specs=pl.BlockSpec((tm, tf), lambda j, i: (i, j)),
        out_shape=jax.ShapeDtypeStruct((m, f), BF16),
        scratch_shapes=[pltpu.VMEM((d, tf), BF16), pltpu.VMEM((d, tf), BF16)],
        compiler_params=_params("arbitrary", "arbitrary"),
        name="ffn_up",
    )(xn, wg, wu)


_SHIFT_ROWS = 256


def _mm_kernel(*refs, act, shift):
    if shift:
        x_ref, w_ref, wn_ref, o_ref, wb = refs
    else:
        x_ref, w_ref, o_ref, wb = refs

    @pl.when(pl.program_id(1) == 0)
    def _():
        if not shift:
            wb[...] = w_ref[...].astype(BF16)
        else:
            k, tn = w_ref.shape
            lane = lax.broadcasted_iota(jnp.int32, (_SHIFT_ROWS, LANES), 1)

            def body(c, carry):
                rows = pl.ds(pl.multiple_of(c * _SHIFT_ROWS, _SHIFT_ROWS), _SHIFT_ROWS)
                a = pltpu.roll(w_ref[rows, :], tn - shift, axis=1)
                b = pltpu.roll(wn_ref[rows, :], LANES - shift, axis=1)
                wb[rows, :tn - LANES] = a[:, :tn - LANES].astype(BF16)
                wb[rows, tn - LANES:] = jnp.where(lane < LANES - shift, a[:, tn - LANES:], b).astype(BF16)
                return carry

            lax.fori_loop(0, k // _SHIFT_ROWS, body, 0)

    y = jnp.dot(x_ref[...], wb[...], preferred_element_type=F32)
    o_ref[...] = act(y).astype(o_ref.dtype)


def _mm(x, w, col_off, n_cols, act, out_dtype, tm=1088, tn=1024):
    m, k = x.shape
    tn = min(tn, n_cols)
    n_tiles = n_cols // tn
    shift = col_off % LANES
    base = col_off - shift
    off = base // tn
    assert off * tn == base and n_tiles * tn == n_cols and k % _SHIFT_ROWS == 0
    in_specs = [pl.BlockSpec((tm, k), lambda j, i: (i, 0)),
                pl.BlockSpec((k, tn), lambda j, i: (0, off + j))]
    args = [x, w]
    if shift:
        per = tn // LANES
        in_specs.append(pl.BlockSpec((k, LANES), lambda j, i: (0, (off + j + 1) * per)))
        args.append(w)
    return pl.pallas_call(
        functools.partial(_mm_kernel, act=act, shift=shift),
        grid=(n_tiles, m // tm),
        in_specs=in_specs,
        out_specs=pl.BlockSpec((tm, tn), lambda j, i: (i, j)),
        out_shape=jax.ShapeDtypeStruct((m, n_cols), out_dtype),
        scratch_shapes=[pltpu.VMEM((k, tn), BF16)],
        compiler_params=_params("arbitrary", "arbitrary"),
        name="mm",
    )(*args)


def _ffn_down_kernel(*refs, n_p):
    if n_p is None:
        h_ref, wd_ref, x_ref, gpost_ref, gnext_ref, hout_ref, xn_ref = refs
        x = x_ref[...]
    else:
        h_ref, wd_ref, xp_ref, xs_ref, gpost_ref, gnext_ref, hout_ref, xn_ref = refs
        x = jnp.where(pl.program_id(0) < n_p, xp_ref[...], xs_ref[...])
    y = jnp.dot(h_ref[...], wd_ref[...], preferred_element_type=F32)
    h = x + 0.5 * _rms(y, gpost_ref[...])
    hout_ref[...] = h
    xn_ref[...] = _rms(h, gnext_ref[...]).astype(xn_ref.dtype)


def _ffn_down(hid, wd, x_parts, g_post, g_next, tm=256):
    m, f = hid.shape
    d = wd.shape[1]
    row = lambda i: (i, 0)
    if len(x_parts) == 1:
        n_p, x_specs = None, [pl.BlockSpec((tm, d), row)]
    else:
        n_p = x_parts[0].shape[0] // tm
        x_specs = _split_specs(tm, d, n_p)
    return pl.pallas_call(
        functools.partial(_ffn_down_kernel, n_p=n_p),
        grid=(m // tm,),
        in_specs=[pl.BlockSpec((tm, f), row), _resident((f, d))] + x_specs
                 + [_resident((1, d)), _resident((1, d))],
        out_specs=[pl.BlockSpec((tm, d), row), pl.BlockSpec((tm, d), row)],
        out_shape=[jax.ShapeDtypeStruct((m, d), F32), jax.ShapeDtypeStruct((m, d), BF16)],
        compiler_params=_params("arbitrary"),
        name="ffn_down",
    )(hid, wd, *x_parts, g_post, g_next)


def _merge_kernel(hap_ref, has_ref, sg_ref, gates_a_ref, gates_b_ref, h_ref, wa_ref, wb_ref, wo_ref,
                  gpost_ref, gnext_ref, hout_ref, xn_ref, *, n_p):
    ha = jnp.where(pl.program_id(0) < n_p, hap_ref[...], has_ref[...])
    ya = jnp.dot(ha, wa_ref[...], preferred_element_type=F32)
    yb = jnp.dot(sg_ref[...], wb_ref[...], preferred_element_type=F32)
    mixin = (gates_a_ref[...] * ya + gates_b_ref[...] * yb).astype(BF16)
    mix = jnp.dot(mixin, wo_ref[...], preferred_element_type=F32)
    h = h_ref[...] + _rms(mix, gpost_ref[...])
    hout_ref[...] = h
    xn_ref[...] = _rms(h, gnext_ref[...]).astype(xn_ref.dtype)


def _merge(ha_p, ha_s, sg, ab_sig, h, wa, wb, wo, g_post, g_next, tm=256):
    m, d = h.shape
    n_p = ha_p.shape[0] // tm
    row = lambda i: (i, 0)
    return pl.pallas_call(
        functools.partial(_merge_kernel, n_p=n_p),
        grid=(m // tm,),
        in_specs=_split_specs(tm, d, n_p)
                 + [pl.BlockSpec((tm, d), row),
                    pl.BlockSpec((tm, d), lambda i: (i, 0)), pl.BlockSpec((tm, d), lambda i: (i, 1)),
                    pl.BlockSpec((tm, d), row),
                    _resident((d, d)), _resident((d, d)), _resident((d, d)),
                    _resident((1, d)), _resident((1, d))],
        out_specs=[pl.BlockSpec((tm, d), row), pl.BlockSpec((tm, d), row)],
        out_shape=[jax.ShapeDtypeStruct((m, d), F32), jax.ShapeDtypeStruct((m, d), BF16)],
        compiler_params=_params("arbitrary"),
        name="merge",
    )(ha_p, ha_s, sg, ab_sig, ab_sig, h, wa, wb, wo, g_post, g_next)


def _ple_kernel(xn_ref, p_ref, h_ref, wg_ref, wu_ref, gpost_ref, outp_ref, outs_ref, *, n_p):
    gate = _sigmoid(jnp.dot(xn_ref[...], wg_ref[...], preferred_element_type=F32))
    up = jnp.dot(p_ref[...].astype(BF16), wu_ref[...], preferred_element_type=F32)
    out = h_ref[...] + _rms(gate * up, gpost_ref[...])
    i = pl.program_id(0)

    @pl.when(i < n_p)
    def _():
        outp_ref[...] = out

    @pl.when(i >= n_p)
    def _():
        outs_ref[...] = out


def _ple(xn, p, h, wg, wu, g_post, mp, tm=256):
    m, d = h.shape
    dp = p.shape[1]
    n_p = mp // tm
    row = lambda i: (i, 0)
    return pl.pallas_call(
        functools.partial(_ple_kernel, n_p=n_p),
        grid=(m // tm,),
        in_specs=[pl.BlockSpec((tm, d), row), pl.BlockSpec((tm, dp), row), pl.BlockSpec((tm, d), row),
                  _resident((d, d)), _resident((dp, d)), _resident((1, d))],
        out_specs=_split_specs(tm, d, n_p),
        out_shape=[jax.ShapeDtypeStruct((mp, d), F32), jax.ShapeDtypeStruct((m - mp, d), F32)],
        compiler_params=_params("arbitrary"),
        name="ple",
    )(xn, p, h, wg, wu, g_post)


def _conv_kernel(x_ref, buf_ref, w_ref, b_ref, q_ref, k_ref, prev, *, ts, n_p, tiles_per_seq, sample_len):
    i = pl.program_id(0)
    w = w_ref[...]
    c = x_ref.shape[1]

    def finish(y):
        s = y * _sigmoid(y)
        q_ref[...] = s[:, :c // 2].astype(q_ref.dtype)
        k_ref[...] = (s[:, c // 2:] * (DK ** -0.5)).astype(k_ref.dtype)

    @pl.when(i < n_p)
    def _():
        @pl.when(lax.rem(i, tiles_per_seq) == 0)
        def _():
            prev[...] = jnp.zeros_like(prev)

        x = x_ref[...]
        tail = prev[...]
        y = b_ref[...] + x * w[CONV_W - 1:CONV_W, :]
        rowid = lax.broadcasted_iota(jnp.int32, (SUBLANES, c), 0)
        for j in range(1, CONV_W):
            xr = pltpu.roll(x, j, axis=0)
            tr = pltpu.roll(tail, j, axis=0)
            top = jnp.where(rowid < j, tr, xr[0:SUBLANES])
            xs = jnp.concatenate([top, xr[SUBLANES:]], axis=0)
            y = y + xs * w[CONV_W - 1 - j:CONV_W - j, :]
        prev[...] = x[ts - SUBLANES:ts]
        finish(y)

    @pl.when(i >= n_p)
    def _():
        x = x_ref[...]
        buf = buf_ref[...]
        t = jnp.bitwise_and(lax.broadcasted_iota(jnp.int32, (ts, c), 0), sample_len - 1)
        y = b_ref[...] + x * w[CONV_W - 1:CONV_W, :]
        for j in range(1, CONV_W):
            back = (j - (CONV_W - 1)) % ts
            from_buf = buf if back == 0 else pltpu.roll(buf, back, axis=0)
            xs = jnp.where(t >= j, pltpu.roll(x, j, axis=0), from_buf)
            y = y + xs * w[CONV_W - 1 - j:CONV_W - j, :]
        finish(y)


def _conv(x, buf, w, b, mp, seq, sample_len, ts=256):
    m, c = x.shape
    n_p = mp // ts
    assert sample_len & (sample_len - 1) == 0 and sample_len >= CONV_W - 1 and ts % sample_len == 0
    kern = functools.partial(_conv_kernel, ts=ts, n_p=n_p, tiles_per_seq=seq // ts, sample_len=sample_len)
    row = lambda i: (i, 0)
    return pl.pallas_call(
        kern,
        grid=(m // ts,),
        in_specs=[pl.BlockSpec((ts, c), row),
                  pl.BlockSpec((ts, c), lambda i: (jnp.maximum(i - n_p, 0), 0)),
                  pl.BlockSpec((CONV_W, c), lambda i: (0, 0)),
                  pl.BlockSpec((1, c), lambda i: (0, 0))],
        out_specs=[pl.BlockSpec((ts, c // 2), row), pl.BlockSpec((ts, c // 2), row)],
        out_shape=[jax.ShapeDtypeStruct((m, c // 2), BF16), jax.ShapeDtypeStruct((m, c // 2), BF16)],
        scratch_shapes=[pltpu.VMEM((SUBLANES, c), F32)],
        compiler_params=_params("arbitrary"),
        name="conv",
    )(x, buf, w, b)


def _head_out(hh, gh, o):
    hn = hh * lax.rsqrt(jnp.mean(hh * hh, axis=1, keepdims=True) + EPS) * gh
    return (o * hn).astype(BF16)


def _mlstm_prompt_kernel(q_ref, k_ref, v_ref, o_ref, g_ref, gb_ref, gh_ref,
                         h_out, c_out, n_out, m_out, c_s, n_s, m_s, *, chunk):
    L = chunk
    head = pl.program_id(1)
    ci = pl.program_id(2)

    @pl.when(ci == 0)
    def _():
        c_s[...] = jnp.zeros_like(c_s)
        n_s[...] = jnp.zeros_like(n_s)
        m_s[...] = jnp.zeros_like(m_s)

    ig, fg = _head_gates(g_ref[...] + gb_ref[...], head)
    lf = _log_sigmoid(fg)

    bc = jnp.broadcast_to(lf, (L, LANES))
    rowi = lax.broadcasted_iota(jnp.int32, (L, LANES), 0)
    sh = 1
    while sh < L:
        bc = bc + jnp.where(rowi >= sh, pltpu.roll(bc, sh, axis=0), 0.0)
        sh *= 2
    bcum = bc[:, 0:1]

    m_prev = m_s[0:1, 0:1]
    m_in = bcum + m_prev
    a_col = ig - bcum
    a_row = _column_to_row(a_col, L)
    r_i = lax.broadcasted_iota(jnp.int32, (L, L), 0)
    c_i = lax.broadcasted_iota(jnp.int32, (L, L), 1)
    d = jnp.where(r_i >= c_i, bcum + a_row, -jnp.inf)
    m_t = jnp.maximum(m_in, jnp.max(d, axis=1, keepdims=True))
    p = jnp.exp(d - m_t)

    qb = q_ref[...]
    kb = k_ref[...]
    vb = v_ref[...]
    s = lax.dot_general(qb, kb, (((1,), (1,)), ((), ())), preferred_element_type=F32) * p
    w_prev = jnp.exp(m_in - m_t)
    cb = c_s[...].astype(BF16)
    num = (jnp.dot(s.astype(BF16), vb, preferred_element_type=F32)
           + w_prev * jnp.dot(qb, cb, preferred_element_type=F32))
    den = (jnp.sum(s, axis=1, keepdims=True)
           + w_prev * jnp.sum(qb.astype(F32) * n_s[...], axis=1, keepdims=True))
    hh = num / jnp.maximum(jnp.abs(den), jnp.exp(-m_t))
    h_out[...] = _head_out(hh, gh_ref[...], o_ref[...])

    m_new = m_t[L - 1:L, :]
    b_last = bcum[L - 1:L, :]
    w_end = jnp.exp(b_last + a_col - m_new)
    decay = jnp.exp(b_last + m_prev - m_new)
    kw = kb.astype(F32) * w_end
    c_s[...] = decay * c_s[...] + lax.dot_general(
        kw.astype(BF16), vb, (((0,), (0,)), ((), ())), preferred_element_type=F32)
    n_s[...] = decay * n_s[...] + jnp.sum(kw, axis=0, keepdims=True)
    m_s[...] = jnp.broadcast_to(m_new, m_s.shape)

    @pl.when(ci == pl.num_programs(2) - 1)
    def _():
        c_out[...] = c_s[...]
        n_out[...] = n_s[...]
        m_out[...] = m_s[...]


def _mlstm_prompt(q, k, v, o, gates, gate_bias, g_head, nb, seq, chunk=MLSTM_CHUNK):
    L = chunk
    nc = seq // L
    rows = lambda b, h, c: (b * nc + c, h)
    state = lambda b, h, c: (b, h, 0, 0)
    return pl.pallas_call(
        functools.partial(_mlstm_prompt_kernel, chunk=L),
        grid=(nb, N_HEADS, nc),
        in_specs=[pl.BlockSpec((L, DK), rows), pl.BlockSpec((L, DK), rows),
                  pl.BlockSpec((L, DV), rows), pl.BlockSpec((L, DV), rows),
                  pl.BlockSpec((L, LANES), lambda b, h, c: (b * nc + c, 0)),
                  pl.BlockSpec((1, LANES), lambda b, h, c: (0, 0)),
                  pl.BlockSpec((1, DV), lambda b, h, c: (0, h))],
        out_specs=[pl.BlockSpec((L, DV), rows),
                   pl.BlockSpec((None, None, DK, DV), state),
                   pl.BlockSpec((None, None, 1, DK), state),
                   pl.BlockSpec((None, None, 1, LANES), state)],
        out_shape=[jax.ShapeDtypeStruct((nb * seq, N_HEADS * DV), BF16),
                   jax.ShapeDtypeStruct((nb, N_HEADS, DK, DV), F32),
                   jax.ShapeDtypeStruct((nb, N_HEADS, 1, DK), F32),
                   jax.ShapeDtypeStruct((nb, N_HEADS, 1, LANES), F32)],
        scratch_shapes=[pltpu.VMEM((DK, DV), F32), pltpu.VMEM((1, DK), F32),
                        pltpu.VMEM((1, LANES), F32)],
        compiler_params=_params("arbitrary", "arbitrary", "arbitrary"),
        name="mlstm_prompt",
    )(q, k, v, o, gates, gate_bias, g_head)


def _mlstm_sample_kernel(q_ref, k_ref, v_ref, o_ref, g_ref, gb_ref, gh_ref, c0_ref, n0_ref, m0_ref,
                         h_out, c_out, n_out, m_out, *, nb, steps):
    R = nb * steps
    log_steps = steps.bit_length() - 1
    head = pl.program_id(1)

    ig, fg = _head_gates(g_ref[...] + gb_ref[...], head)
    lf = _log_sigmoid(fg)

    bc = jnp.broadcast_to(lf, (R, LANES))
    tt = jnp.bitwise_and(lax.broadcasted_iota(jnp.int32, (R, LANES), 0), steps - 1)
    sh = 1
    while sh < steps:
        bc = bc + jnp.where(tt >= sh, pltpu.roll(bc, sh, axis=0), 0.0)
        sh *= 2
    bcum = bc[:, 0:1]

    seq_of_row = lax.shift_right_logical(lax.broadcasted_iota(jnp.int32, (R, 1), 0), log_steps)

    def per_row(vals):
        out = jnp.broadcast_to(vals[0], (R, 1))
        for b in range(1, nb):
            out = jnp.where(seq_of_row == b, vals[b], out)
        return out

    m0 = [m0_ref[b][0:1, 0:1] for b in range(nb)]
    m_in = bcum + per_row(m0)
    a_col = ig - bcum
    a_row = _column_to_row(a_col, R)
    r_i = lax.broadcasted_iota(jnp.int32, (R, R), 0)
    c_i = lax.broadcasted_iota(jnp.int32, (R, R), 1)
    same_seq = lax.shift_right_logical(r_i, log_steps) == lax.shift_right_logical(c_i, log_steps)
    d = jnp.where(same_seq, jnp.where(r_i >= c_i, bcum + a_row, -jnp.inf), -jnp.inf)
    m_t = jnp.maximum(m_in, jnp.max(d, axis=1, keepdims=True))
    p = jnp.exp(d - m_t)

    qb = q_ref[...]
    kb = k_ref[...]
    vb = v_ref[...]
    s = lax.dot_general(qb, kb, (((1,), (1,)), ((), ())), preferred_element_type=F32) * p
    w_prev = jnp.exp(m_in - m_t)
    inter = jnp.zeros((R, DV), F32)
    n_rows = jnp.zeros((R, DK), F32)
    for b in range(nb):
        qc = jnp.dot(qb, c0_ref[b].astype(BF16), preferred_element_type=F32)
        inter = jnp.where(seq_of_row == b, qc, inter)
        n_rows = jnp.where(seq_of_row == b, n0_ref[b], n_rows)
    num = jnp.dot(s.astype(BF16), vb, preferred_element_type=F32) + w_prev * inter
    den = (jnp.sum(s, axis=1, keepdims=True)
           + w_prev * jnp.sum(qb.astype(F32) * n_rows, axis=1, keepdims=True))
    hh = num / jnp.maximum(jnp.abs(den), jnp.exp(-m_t))
    h_out[...] = _head_out(hh, gh_ref[...], o_ref[...])

    last = [b * steps + steps - 1 for b in range(nb)]
    m_new = [m_t[r:r + 1, :] for r in last]
    b_last = [bcum[r:r + 1, :] for r in last]
    w_end = jnp.exp(per_row(b_last) + a_col - per_row(m_new))
    kw = kb.astype(F32) * w_end
    for b in range(nb):
        decay = jnp.exp(b_last[b] + m0[b] - m_new[b])
        kw_b = jnp.where(seq_of_row == b, kw, 0.0)
        c_out[b] = decay * c0_ref[b] + lax.dot_general(
            kw_b.astype(BF16), vb, (((0,), (0,)), ((), ())), preferred_element_type=F32)
        n_out[b] = decay * n0_ref[b] + jnp.sum(kw_b, axis=0, keepdims=True)
        m_out[b] = jnp.broadcast_to(m_new[b], (1, LANES))


def _mlstm_sample(q, k, v, o, gates, gate_bias, g_head, c0, n0, m0, row0, steps):
    nbs = c0.shape[0]
    nb = BF16_ROWS // steps
    R = nb * steps
    assert steps & (steps - 1) == 0 and nb * steps == BF16_ROWS and row0 % R == 0 and nbs % nb == 0
    r0 = row0 // R
    rows = lambda i, h: (r0 + i, h)
    out_rows = lambda i, h: (i, h)
    state = lambda i, h: (i, h, 0, 0)
    return pl.pallas_call(
        functools.partial(_mlstm_sample_kernel, nb=nb, steps=steps),
        grid=(nbs // nb, N_HEADS),
        in_specs=[pl.BlockSpec((R, DK), rows), pl.BlockSpec((R, DK), rows),
                  pl.BlockSpec((R, DV), rows), pl.BlockSpec((R, DV), rows),
                  pl.BlockSpec((R, LANES), lambda i, h: (r0 + i, 0)),
                  pl.BlockSpec((1, LANES), lambda i, h: (0, 0)),
                  pl.BlockSpec((1, DV), lambda i, h: (0, h)),
                  pl.BlockSpec((nb, None, DK, DV), state),
                  pl.BlockSpec((nb, None, 1, DK), state),
                  pl.BlockSpec((nb, None, 1, LANES), state)],
        out_specs=[pl.BlockSpec((R, DV), out_rows),
                   pl.BlockSpec((nb, None, DK, DV), state),
                   pl.BlockSpec((nb, None, 1, DK), state),
                   pl.BlockSpec((nb, None, 1, LANES), state)],
        out_shape=[jax.ShapeDtypeStruct((nbs * steps, N_HEADS * DV), BF16),
                   jax.ShapeDtypeStruct((nbs, N_HEADS, DK, DV), F32),
                   jax.ShapeDtypeStruct((nbs, N_HEADS, 1, DK), F32),
                   jax.ShapeDtypeStruct((nbs, N_HEADS, 1, LANES), F32)],
        compiler_params=_params("arbitrary", "arbitrary"),
        name="mlstm_sample",
    )(q, k, v, o, gates, gate_bias, g_head, c0, n0, m0)


def _gmlp_kernel(u_ref, v_ref, gln_ref, bln_ref, wsp_ref, bsp_ref, coef_ref, bias_ref,
                 out_ref, vg_ref, *, n_prompt_tiles, sample_len):
    x = v_ref[...]
    mu = jnp.mean(x, axis=-1, keepdims=True)
    xc = x - mu
    var = jnp.mean(xc * xc, axis=-1, keepdims=True)
    vg = xc * lax.rsqrt(var + EPS) * gln_ref[...] + bln_ref[...]
    u = u_ref[...]
    rows, width = x.shape
    gw = width // N_GROUPS
    is_prompt = pl.program_id(0) < n_prompt_tiles

    @pl.when(is_prompt)
    def _():
        r_i = lax.broadcasted_iota(jnp.int32, (rows, rows), 0)
        c_i = lax.broadcasted_iota(jnp.int32, (rows, rows), 1)
        for g in range(N_GROUPS):
            w = jnp.where(r_i >= c_i, wsp_ref[g], 0.0).astype(BF16)
            sl = slice(g * gw, (g + 1) * gw)
            mixed = jnp.dot(w, vg[:, sl].astype(BF16), preferred_element_type=F32) + bsp_ref[:, g:g + 1]
            out_ref[:, sl] = (u[:, sl] * mixed).astype(out_ref.dtype)

    @pl.when(jnp.logical_not(is_prompt))
    def _():
        vg_ref[...] = vg
        for g in range(N_GROUPS):
            sl = slice(g * gw, (g + 1) * gw)
            vgg = vg[:, sl]
            acc = bias_ref[:, g:g + 1] + coef_ref[:, g * sample_len:g * sample_len + 1] * vgg
            for j in range(1, sample_len):
                cj = coef_ref[:, g * sample_len + j:g * sample_len + j + 1]
                acc = acc + cj * pltpu.roll(vgg, j, axis=0)
            out_ref[:, sl] = (u[:, sl] * acc).astype(out_ref.dtype)


def _gmlp(uv, g_ln, b_ln, w_sp, b_sp_t, coef, bias, n_prompt_rows, n_sample_rows, sample_len):
    width = g_ln.shape[1]
    tr = GMLP_CHUNK
    n_pt = n_prompt_rows // tr
    n_st = n_sample_rows // tr
    kern = functools.partial(_gmlp_kernel, n_prompt_tiles=n_pt, sample_len=sample_len)
    const2 = lambda i: (0, 0)
    return pl.pallas_call(
        kern,
        grid=(n_pt + n_st,),
        in_specs=[pl.BlockSpec((tr, width), lambda i: (i, 0)),
                  pl.BlockSpec((tr, width), lambda i: (i, 1)),
                  pl.BlockSpec((1, width), const2), pl.BlockSpec((1, width), const2),
                  pl.BlockSpec((N_GROUPS, tr, tr), lambda i: (0, 0, 0)),
                  pl.BlockSpec((tr, N_GROUPS), const2),
                  pl.BlockSpec((tr, N_GROUPS * sample_len), const2),
                  pl.BlockSpec((tr, N_GROUPS), const2)],
        out_specs=[pl.BlockSpec((tr, width), lambda i: (i, 0)),
                   pl.BlockSpec((tr, width), lambda i: (jnp.maximum(i - n_pt, 0), 0))],
        out_shape=[jax.ShapeDtypeStruct((n_prompt_rows + n_sample_rows, width), BF16),
                   jax.ShapeDtypeStruct((n_sample_rows, width), F32)],
        compiler_params=_params("arbitrary"),
        name="gmlp",
    )(uv, uv, g_ln, b_ln, w_sp, b_sp_t, coef, bias)


def kernel(x_prompt, x_sample, p_prompt, p_sample, state_mlstm_conv, state_mlstm_C, state_mlstm_n, state_mlstm_m, g_ffn1_pre, w_ffn1_gate, w_ffn1_up, w_ffn1_down, g_ffn1_post, g_mix_pre, w_in, w_conv, b_conv, b_igate, b_fgate, g_head, w_a_out, g_ln_v, b_ln_v, w_spatial, b_spatial, w_b_out, w_o, g_mix_post, g_ffn2_pre, w_ffn2_gate, w_ffn2_up, w_ffn2_down, g_ffn2_post, g_ple_pre, w_ple_gate, w_ple_up, g_ple_post):
    assert w_in.shape[0] == 1, "single layer"
    nbp, seq, d = x_prompt.shape
    nbs, sseq, _ = x_sample.shape
    mp, ms = nbp * seq, nbs * sseq
    qk_w = 2 * N_HEADS * DK
    v_w = N_HEADS * DV
    d_b = g_ln_v.shape[1]
    gates_off = qk_w + 2 * v_w
    tail_off = gates_off + 2 * N_HEADS

    x_p, x_s = x_prompt.reshape(mp, d), x_sample.reshape(ms, d)
    p_all = jnp.concatenate([p_prompt[0].reshape(mp, -1), p_sample[0].reshape(ms, -1)], axis=0)

    xn1 = _rms_cast(x_p, x_s, g_ffn1_pre)
    hid1 = _ffn_up(xn1, w_ffn1_gate[0], w_ffn1_up[0])
    h1, xn2 = _ffn_down(hid1, w_ffn1_down[0].astype(BF16), (x_p, x_s), g_ffn1_post, g_mix_pre)

    w_in0 = w_in[0]
    qk_pre = _mm(xn2, w_in0, 0, qk_w, _identity, F32)
    v_a = _mm(xn2, w_in0, qk_w, v_w, _identity, BF16)
    o_sig = _mm(xn2, w_in0, qk_w + v_w, v_w, _sigmoid, F32)
    gates = _mm(xn2, w_in0, gates_off, LANES, _identity, F32)
    uv_gelu = _mm(xn2, w_in0, tail_off, 2 * d_b, _gelu, F32)
    ab_sig = _mm(xn2, w_in0, tail_off + 2 * d_b, 2 * d, _sigmoid, F32)
    gate_bias = jnp.pad(jnp.concatenate([b_igate[0], b_fgate[0]]), (0, LANES - 2 * N_HEADS)).reshape(1, LANES)

    buf = jnp.pad(state_mlstm_conv[0], ((0, 0), (0, sseq - (CONV_W - 1)), (0, 0))).reshape(ms, qk_w)
    q_all, k_all = _conv(qk_pre, buf, w_conv[0], b_conv, mp, seq, sseq)
    ha_p, c_p, n_p, m_p = _mlstm_prompt(q_all, k_all, v_a, o_sig, gates, gate_bias, g_head, nbp, seq)
    init = (state_mlstm_C[0], state_mlstm_n[0].reshape(nbs, N_HEADS, 1, DK),
            jnp.broadcast_to(state_mlstm_m[0][:, :, None, None], (nbs, N_HEADS, 1, LANES)))
    ha_s, c_s, n_s, m_s = _mlstm_sample(q_all, k_all, v_a, o_sig, gates, gate_bias, g_head, *init, mp, sseq)

    t_idx = jnp.arange(GMLP_CHUNK) % sseq
    src = t_idx[:, None] - jnp.arange(sseq)[None, :]
    w_small = w_spatial[0][:, :sseq, :sseq]
    coef = jnp.where(src[None] >= 0, w_small[:, t_idx[:, None], jnp.maximum(src, 0)], 0.0)
    coef = coef.transpose(1, 0, 2).reshape(GMLP_CHUNK, N_GROUPS * sseq)
    bias_s = b_spatial[0][:, t_idx].T
    sg_all, vg_s = _gmlp(uv_gelu, g_ln_v, b_ln_v, w_spatial[0], b_spatial[0].T, coef, bias_s, mp, ms, sseq)

    h2, xn3 = _merge(ha_p, ha_s, sg_all, ab_sig, h1, w_a_out[0].astype(BF16), w_b_out[0].astype(BF16),
                     w_o[0].astype(BF16), g_mix_post, g_ffn2_pre)
    hid2 = _ffn_up(xn3, w_ffn2_gate[0], w_ffn2_up[0])
    h3, xn4 = _ffn_down(hid2, w_ffn2_down[0].astype(BF16), (h2,), g_ffn2_post, g_ple_pre)
    y_p, y_s = _ple(xn4, p_all, h3, w_ple_gate[0].astype(BF16), w_ple_up[0].astype(BF16), g_ple_post, mp)

    keep = CONV_W - 1
    conv_prompt = jnp.stack([qk_pre[(b + 1) * seq - keep:(b + 1) * seq] for b in range(nbp)])
    conv_sample = qk_pre[mp:].reshape(nbs, sseq, qk_w)[:, sseq - keep:]
    return (y_p.reshape(nbp, seq, d), y_s.reshape(nbs, sseq, d),
            conv_prompt[None], c_p[None], n_p.reshape(1, nbp, N_HEADS, DK), m_p[:, :, 0, 0][None],
            conv_sample[None], c_s[None], n_s.reshape(1, nbs, N_HEADS, DK), m_s[:, :, 0, 0][None],
            vg_s.reshape(1, nbs, sseq, d_b))
```

```python
import functools

import jax
import jax.numpy as jnp
from jax import lax
from jax.experimental import pallas as pl
from jax.experimental.pallas import tpu as pltpu

F32 = jnp.float32
BF16 = jnp.bfloat16
EPS = 1e-6

N_HEADS = 4
DK = 256
DV = 512
CONV_W = 4
N_GROUPS = 4
GMLP_CHUNK = 128
LANES = 128
SUBLANES = 8
BF16_ROWS = 16
MLSTM_CHUNK = 256
VMEM_LIMIT = 54 * 1024 * 1024


def _params(*sem):
    return pltpu.CompilerParams(dimension_semantics=sem, vmem_limit_bytes=VMEM_LIMIT)


def _rms(x, g):
    return x * lax.rsqrt(jnp.mean(x * x, axis=-1, keepdims=True) + EPS) * g


def _gelu(x):
    return 0.5 * x * (1.0 + lax.erf(x * 0.7071067811865476))


def _sigmoid(x):
    return jax.nn.sigmoid(x)


def _identity(x):
    return x


def _log_sigmoid(x):
    return jnp.minimum(x, 0.0) - jnp.log1p(jnp.exp(-jnp.abs(x)))


def _resident(shape):
    return pl.BlockSpec(shape, lambda *_: (0,) * len(shape), pipeline_mode=pl.Buffered(1))


def _split_specs(tm, width, n_p):
    return [pl.BlockSpec((tm, width), lambda i: (jnp.minimum(i, n_p - 1), 0)),
            pl.BlockSpec((tm, width), lambda i: (jnp.maximum(i - n_p, 0), 0))]


def _head_gates(g, head):
    lane = lax.broadcasted_iota(jnp.int32, g.shape, 1)
    ig = jnp.sum(jnp.where(lane == head, g, 0.0), axis=1, keepdims=True)
    fg = jnp.sum(jnp.where(lane == head + N_HEADS, g, 0.0), axis=1, keepdims=True)
    return ig, fg


def _column_to_row(col, n):
    r_i = lax.broadcasted_iota(jnp.int32, (n, n), 0)
    c_i = lax.broadcasted_iota(jnp.int32, (n, n), 1)
    return jnp.sum(jnp.where(r_i == c_i, jnp.broadcast_to(col, (n, n)), 0.0), axis=0, keepdims=True)


def _rms_cast_kernel(xp_ref, xs_ref, g_ref, o_ref, *, n_p):
    i = pl.program_id(0)

    @pl.when(i < n_p)
    def _():
        o_ref[...] = _rms(xp_ref[...], g_ref[...]).astype(o_ref.dtype)

    @pl.when(i >= n_p)
    def _():
        o_ref[...] = _rms(xs_ref[...], g_ref[...]).astype(o_ref.dtype)


def _rms_cast(x_p, x_s, g, tm=512):
    (mp, d), ms = x_p.shape, x_s.shape[0]
    n_p = mp // tm
    return pl.pallas_call(
        functools.partial(_rms_cast_kernel, n_p=n_p),
        grid=((mp + ms) // tm,),
        in_specs=_split_specs(tm, d, n_p) + [pl.BlockSpec((1, d), lambda i: (0, 0))],
        out_specs=pl.BlockSpec((tm, d), lambda i: (i, 0)),
        out_shape=jax.ShapeDtypeStruct((mp + ms, d), BF16),
        compiler_params=_params("arbitrary"),
        name="rms_cast",
    )(x_p, x_s, g)


def _side_specs(side, n_steps, n_inner):
    rows, cols = side.shape
    slab = rows // n_steps
    assert slab * n_steps == rows and slab % BF16_ROWS == 0
    spec = pl.BlockSpec((slab, cols), lambda j, i: (j * n_inner + i, 0))
    return spec, jax.ShapeDtypeStruct((rows, cols), BF16)


def _ffn_up_kernel(x_ref, wg_ref, wu_ref, side_ref, o_ref, side_out, wgb, wub):
    @pl.when(pl.program_id(1) == 0)
    def _():
        wgb[...] = wg_ref[...].astype(BF16)
        wub[...] = wu_ref[...].astype(BF16)

    side_out[...] = side_ref[...].astype(BF16)
    x = x_ref[...]
    g = jnp.dot(x, wgb[...], preferred_element_type=F32)
    u = jnp.dot(x, wub[...], preferred_element_type=F32)
    o_ref[...] = (g * _sigmoid(g) * u).astype(o_ref.dtype)


def _ffn_up(xn, wg, wu, side, tm=1088, tf=512):
    m, d = xn.shape
    f = wg.shape[1]
    grid = (f // tf, m // tm)
    side_spec, side_shape = _side_specs(side, grid[0] * grid[1], grid[1])
    return pl.pallas_call(
        _ffn_up_kernel,
        grid=grid,
        in_specs=[pl.BlockSpec((tm, d), lambda j, i: (i, 0)),
                  pl.BlockSpec((d, tf), lambda j, i: (0, j)),
                  pl.BlockSpec((d, tf), lambda j, i: (0, j)),
                  side_spec],
        out_specs=[pl.BlockSpec((tm, tf), lambda j, i: (i, j)), side_spec],
        out_shape=[jax.ShapeDtypeStruct((m, f), BF16), side_shape],
        scratch_shapes=[pltpu.VMEM((d, tf), BF16), pltpu.VMEM((d, tf), BF16)],
        compiler_params=_params("arbitrary", "arbitrary"),
        name="ffn_up",
    )(xn, wg, wu, side)


def _mm_nt_kernel(*refs, act, has_side):
    if has_side:
        x_ref, w_ref, side_ref, o_ref, side_out, wb = refs
        side_out[...] = side_ref[...].astype(BF16)
    else:
        x_ref, w_ref, o_ref, wb = refs

    @pl.when(pl.program_id(1) == 0)
    def _():
        wb[...] = w_ref[...].astype(BF16)

    y = lax.dot_general(x_ref[...], wb[...], (((1,), (1,)), ((), ())), preferred_element_type=F32)
    o_ref[...] = act(y).astype(o_ref.dtype)


def _mm_nt(x, wt, row_off, n_out, act, out_dtype, side=None, tm=1088, tn=1024):
    m, k = x.shape
    tn = min(tn, n_out)
    grid = (n_out // tn, m // tm)
    assert grid[0] * tn == n_out and row_off % SUBLANES == 0
    if row_off % tn == 0:
        w_spec = pl.BlockSpec((tn, k), lambda j, i: (row_off // tn + j, 0))
    else:
        w_spec = pl.BlockSpec((pl.Element(tn), pl.Element(k)),
                              lambda j, i: (pl.multiple_of(row_off + j * tn, SUBLANES), 0))
    in_specs = [pl.BlockSpec((tm, k), lambda j, i: (i, 0)), w_spec]
    out_specs = [pl.BlockSpec((tm, tn), lambda j, i: (i, j))]
    out_shape = [jax.ShapeDtypeStruct((m, n_out), out_dtype)]
    args = [x, wt]
    if side is not None:
        side_spec, side_shape = _side_specs(side, grid[0] * grid[1], grid[1])
        in_specs.append(side_spec)
        out_specs.append(side_spec)
        out_shape.append(side_shape)
        args.append(side)
    out = pl.pallas_call(
        functools.partial(_mm_nt_kernel, act=act, has_side=side is not None),
        grid=grid,
        in_specs=in_specs,
        out_specs=out_specs,
        out_shape=out_shape,
        scratch_shapes=[pltpu.VMEM((tn, k), BF16)],
        compiler_params=_params("arbitrary", "arbitrary"),
        name="mm_nt",
    )(*args)
    return out if side is not None else out[0]


def _ffn_down_kernel(*refs, n_p):
    if n_p is None:
        h_ref, wd_ref, x_ref, gpost_ref, gnext_ref, hout_ref, xn_ref = refs
        x = x_ref[...]
    else:
        h_ref, wd_ref, xp_ref, xs_ref, gpost_ref, gnext_ref, hout_ref, xn_ref = refs
        x = jnp.where(pl.program_id(0) < n_p, xp_ref[...], xs_ref[...])
    y = jnp.dot(h_ref[...], wd_ref[...], preferred_element_type=F32)
    h = x + 0.5 * _rms(y, gpost_ref[...])
    hout_ref[...] = h
    xn_ref[...] = _rms(h, gnext_ref[...]).astype(xn_ref.dtype)


def _ffn_down(hid, wd, x_parts, g_post, g_next, tm=256):
    m, f = hid.shape
    d = wd.shape[1]
    row = lambda i: (i, 0)
    if len(x_parts) == 1:
        n_p, x_specs = None, [pl.BlockSpec((tm, d), row)]
    else:
        n_p = x_parts[0].shape[0] // tm
        x_specs = _split_specs(tm, d, n_p)
    return pl.pallas_call(
        functools.partial(_ffn_down_kernel, n_p=n_p),
        grid=(m // tm,),
        in_specs=[pl.BlockSpec((tm, f), row), _resident((f, d))] + x_specs
                 + [_resident((1, d)), _resident((1, d))],
        out_specs=[pl.BlockSpec((tm, d), row), pl.BlockSpec((tm, d), row)],
        out_shape=[jax.ShapeDtypeStruct((m, d), F32), jax.ShapeDtypeStruct((m, d), BF16)],
        compiler_params=_params("arbitrary"),
        name="ffn_down",
    )(hid, wd, *x_parts, g_post, g_next)


def _merge_kernel(hap_ref, has_ref, sg_ref, gates_a_ref, gates_b_ref, h_ref, wa_ref, wb_ref, wo_ref,
                  gpost_ref, gnext_ref, hout_ref, xn_ref, *, n_p):
    ha = jnp.where(pl.program_id(0) < n_p, hap_ref[...], has_ref[...])
    ya = jnp.dot(ha, wa_ref[...], preferred_element_type=F32)
    yb = jnp.dot(sg_ref[...], wb_ref[...], preferred_element_type=F32)
    mixin = (gates_a_ref[...] * ya + gates_b_ref[...] * yb).astype(BF16)
    mix = jnp.dot(mixin, wo_ref[...], preferred_element_type=F32)
    h = h_ref[...] + _rms(mix, gpost_ref[...])
    hout_ref[...] = h
    xn_ref[...] = _rms(h, gnext_ref[...]).astype(xn_ref.dtype)


def _merge(ha_p, ha_s, sg, ab_sig, h, wa, wb, wo, g_post, g_next, tm=256):
    m, d = h.shape
    n_p = ha_p.shape[0] // tm
    row = lambda i: (i, 0)
    return pl.pallas_call(
        functools.partial(_merge_kernel, n_p=n_p),
        grid=(m // tm,),
        in_specs=_split_specs(tm, d, n_p)
                 + [pl.BlockSpec((tm, d), row),
                    pl.BlockSpec((tm, d), lambda i: (i, 0)), pl.BlockSpec((tm, d), lambda i: (i, 1)),
                    pl.BlockSpec((tm, d), row),
                    _resident((d, d)), _resident((d, d)), _resident((d, d)),
                    _resident((1, d)), _resident((1, d))],
        out_specs=[pl.BlockSpec((tm, d), row), pl.BlockSpec((tm, d), row)],
        out_shape=[jax.ShapeDtypeStruct((m, d), F32), jax.ShapeDtypeStruct((m, d), BF16)],
        compiler_params=_params("arbitrary"),
        name="merge",
    )(ha_p, ha_s, sg, ab_sig, ab_sig, h, wa, wb, wo, g_post, g_next)


def _ple_kernel(xn_ref, p_ref, h_ref, wg_ref, wu_ref, gpost_ref, outp_ref, outs_ref, *, n_p):
    gate = _sigmoid(jnp.dot(xn_ref[...], wg_ref[...], preferred_element_type=F32))
    up = jnp.dot(p_ref[...].astype(BF16), wu_ref[...], preferred_element_type=F32)
    out = h_ref[...] + _rms(gate * up, gpost_ref[...])
    i = pl.program_id(0)

    @pl.when(i < n_p)
    def _():
        outp_ref[...] = out

    @pl.when(i >= n_p)
    def _():
        outs_ref[...] = out


def _ple(xn, p, h, wg, wu, g_post, mp, tm=256):
    m, d = h.shape
    dp = p.shape[1]
    n_p = mp // tm
    row = lambda i: (i, 0)
    return pl.pallas_call(
        functools.partial(_ple_kernel, n_p=n_p),
        grid=(m // tm,),
        in_specs=[pl.BlockSpec((tm, d), row), pl.BlockSpec((tm, dp), row), pl.BlockSpec((tm, d), row),
                  _resident((d, d)), _resident((dp, d)), _resident((1, d))],
        out_specs=_split_specs(tm, d, n_p),
        out_shape=[jax.ShapeDtypeStruct((mp, d), F32), jax.ShapeDtypeStruct((m - mp, d), F32)],
        compiler_params=_params("arbitrary"),
        name="ple",
    )(xn, p, h, wg, wu, g_post)


def _conv_kernel(x_ref, buf_ref, w_ref, b_ref, q_ref, k_ref, prev, *, ts, n_p, tiles_per_seq, sample_len):
    i = pl.program_id(0)
    w = w_ref[...]
    c = x_ref.shape[1]

    def finish(y):
        s = y * _sigmoid(y)
        q_ref[...] = s[:, :c // 2].astype(q_ref.dtype)
        k_ref[...] = (s[:, c // 2:] * (DK ** -0.5)).astype(k_ref.dtype)

    @pl.when(i < n_p)
    def _():
        @pl.when(lax.rem(i, tiles_per_seq) == 0)
        def _():
            prev[...] = jnp.zeros_like(prev)

        x = x_ref[...]
        tail = prev[...]
        y = b_ref[...] + x * w[CONV_W - 1:CONV_W, :]
        rowid = lax.broadcasted_iota(jnp.int32, (SUBLANES, c), 0)
        for j in range(1, CONV_W):
            xr = pltpu.roll(x, j, axis=0)
            tr = pltpu.roll(tail, j, axis=0)
            top = jnp.where(rowid < j, tr, xr[0:SUBLANES])
            xs = jnp.concatenate([top, xr[SUBLANES:]], axis=0)
            y = y + xs * w[CONV_W - 1 - j:CONV_W - j, :]
        prev[...] = x[ts - SUBLANES:ts]
        finish(y)

    @pl.when(i >= n_p)
    def _():
        x = x_ref[...]
        buf = buf_ref[...]
        t = jnp.bitwise_and(lax.broadcasted_iota(jnp.int32, (ts, c), 0), sample_len - 1)
        y = b_ref[...] + x * w[CONV_W - 1:CONV_W, :]
        for j in range(1, CONV_W):
            back = (j - (CONV_W - 1)) % ts
            from_buf = buf if back == 0 else pltpu.roll(buf, back, axis=0)
            xs = jnp.where(t >= j, pltpu.roll(x, j, axis=0), from_buf)
            y = y + xs * w[CONV_W - 1 - j:CONV_W - j, :]
        finish(y)


def _conv(x, buf, w, b, mp, seq, sample_len, ts=256):
    m, c = x.shape
    n_p = mp // ts
    assert sample_len & (sample_len - 1) == 0 and sample_len >= CONV_W - 1 and ts % sample_len == 0
    kern = functools.partial(_conv_kernel, ts=ts, n_p=n_p, tiles_per_seq=seq // ts, sample_len=sample_len)
    row = lambda i: (i, 0)
    return pl.pallas_call(
        kern,
        grid=(m // ts,),
        in_specs=[pl.BlockSpec((ts, c), row),
                  pl.BlockSpec((ts, c), lambda i: (jnp.maximum(i - n_p, 0), 0)),
                  pl.BlockSpec((CONV_W, c), lambda i: (0, 0)),
                  pl.BlockSpec((1, c), lambda i: (0, 0))],
        out_specs=[pl.BlockSpec((ts, c // 2), row), pl.BlockSpec((ts, c // 2), row)],
        out_shape=[jax.ShapeDtypeStruct((m, c // 2), BF16), jax.ShapeDtypeStruct((m, c // 2), BF16)],
        scratch_shapes=[pltpu.VMEM((SUBLANES, c), F32)],
        compiler_params=_params("arbitrary"),
        name="conv",
    )(x, buf, w, b)


def _head_out(hh, gh, o):
    hn = hh * lax.rsqrt(jnp.mean(hh * hh, axis=1, keepdims=True) + EPS) * gh
    return (o * hn).astype(BF16)


def _mlstm_prompt_kernel(q_ref, k_ref, v_ref, o_ref, g_ref, gb_ref, gh_ref,
                         h_out, c_out, n_out, m_out, c_s, n_s, m_s, *, chunk):
    L = chunk
    head = pl.program_id(1)
    ci = pl.program_id(2)

    @pl.when(ci == 0)
    def _():
        c_s[...] = jnp.zeros_like(c_s)
        n_s[...] = jnp.zeros_like(n_s)
        m_s[...] = jnp.zeros_like(m_s)

    ig, fg = _head_gates(g_ref[...] + gb_ref[...], head)
    lf = _log_sigmoid(fg)

    bc = jnp.broadcast_to(lf, (L, LANES))
    rowi = lax.broadcasted_iota(jnp.int32, (L, LANES), 0)
    sh = 1
    while sh < L:
        bc = bc + jnp.where(rowi >= sh, pltpu.roll(bc, sh, axis=0), 0.0)
        sh *= 2
    bcum = bc[:, 0:1]

    m_prev = m_s[0:1, 0:1]
    m_in = bcum + m_prev
    a_col = ig - bcum
    a_row = _column_to_row(a_col, L)
    r_i = lax.broadcasted_iota(jnp.int32, (L, L), 0)
    c_i = lax.broadcasted_iota(jnp.int32, (L, L), 1)
    d = jnp.where(r_i >= c_i, bcum + a_row, -jnp.inf)
    m_t = jnp.maximum(m_in, jnp.max(d, axis=1, keepdims=True))
    p = jnp.exp(d - m_t)

    qb = q_ref[...]
    kb = k_ref[...]
    vb = v_ref[...]
    s = lax.dot_general(qb, kb, (((1,), (1,)), ((), ())), preferred_element_type=F32) * p
    w_prev = jnp.exp(m_in - m_t)
    cb = c_s[...].astype(BF16)
    num = (jnp.dot(s.astype(BF16), vb, preferred_element_type=F32)
           + w_prev * jnp.dot(qb, cb, preferred_element_type=F32))
    den = (jnp.sum(s, axis=1, keepdims=True)
           + w_prev * jnp.sum(qb.astype(F32) * n_s[...], axis=1, keepdims=True))
    hh = num / jnp.maximum(jnp.abs(den), jnp.exp(-m_t))
    h_out[...] = _head_out(hh, gh_ref[...], o_ref[...])

    m_new = m_t[L - 1:L, :]
    b_last = bcum[L - 1:L, :]
    w_end = jnp.exp(b_last + a_col - m_new)
    decay = jnp.exp(b_last + m_prev - m_new)
    kw = kb.astype(F32) * w_end
    c_s[...] = decay * c_s[...] + lax.dot_general(
        kw.astype(BF16), vb, (((0,), (0,)), ((), ())), preferred_element_type=F32)
    n_s[...] = decay * n_s[...] + jnp.sum(kw, axis=0, keepdims=True)
    m_s[...] = jnp.broadcast_to(m_new, m_s.shape)

    @pl.when(ci == pl.num_programs(2) - 1)
    def _():
        c_out[...] = c_s[...]
        n_out[...] = n_s[...]
        m_out[...] = m_s[...]


def _mlstm_prompt(q, k, v, o, gates, gate_bias, g_head, nb, seq, chunk=MLSTM_CHUNK):
    L = chunk
    nc = seq // L
    rows = lambda b, h, c: (b * nc + c, h)
    state = lambda b, h, c: (b, h, 0, 0)
    return pl.pallas_call(
        functools.partial(_mlstm_prompt_kernel, chunk=L),
        grid=(nb, N_HEADS, nc),
        in_specs=[pl.BlockSpec((L, DK), rows), pl.BlockSpec((L, DK), rows),
                  pl.BlockSpec((L, DV), rows), pl.BlockSpec((L, DV), rows),
                  pl.BlockSpec((L, LANES), lambda b, h, c: (b * nc + c, 0)),
                  pl.BlockSpec((1, LANES), lambda b, h, c: (0, 0)),
                  pl.BlockSpec((1, DV), lambda b, h, c: (0, h))],
        out_specs=[pl.BlockSpec((L, DV), rows),
                   pl.BlockSpec((None, None, DK, DV), state),
                   pl.BlockSpec((None, None, 1, DK), state),
                   pl.BlockSpec((None, None, 1, LANES), state)],
        out_shape=[jax.ShapeDtypeStruct((nb * seq, N_HEADS * DV), BF16),
                   jax.ShapeDtypeStruct((nb, N_HEADS, DK, DV), F32),
                   jax.ShapeDtypeStruct((nb, N_HEADS, 1, DK), F32),
                   jax.ShapeDtypeStruct((nb, N_HEADS, 1, LANES), F32)],
        scratch_shapes=[pltpu.VMEM((DK, DV), F32), pltpu.VMEM((1, DK), F32),
                        pltpu.VMEM((1, LANES), F32)],
        compiler_params=_params("arbitrary", "arbitrary", "arbitrary"),
        name="mlstm_prompt",
    )(q, k, v, o, gates, gate_bias, g_head)


def _mlstm_sample_kernel(q_ref, k_ref, v_ref, o_ref, g_ref, gb_ref, gh_ref, c0_ref, n0_ref, m0_ref,
                         h_out, c_out, n_out, m_out, *, nb, steps):
    R = nb * steps
    log_steps = steps.bit_length() - 1
    head = pl.program_id(1)

    ig, fg = _head_gates(g_ref[...] + gb_ref[...], head)
    lf = _log_sigmoid(fg)

    bc = jnp.broadcast_to(lf, (R, LANES))
    tt = jnp.bitwise_and(lax.broadcasted_iota(jnp.int32, (R, LANES), 0), steps - 1)
    sh = 1
    while sh < steps:
        bc = bc + jnp.where(tt >= sh, pltpu.roll(bc, sh, axis=0), 0.0)
        sh *= 2
    bcum = bc[:, 0:1]

    seq_of_row = lax.shift_right_logical(lax.broadcasted_iota(jnp.int32, (R, 1), 0), log_steps)

    def per_row(vals):
        out = jnp.broadcast_to(vals[0], (R, 1))
        for b in range(1, nb):
            out = jnp.where(seq_of_row == b, vals[b], out)
        return out

    m0 = [m0_ref[b][0:1, 0:1] for b in range(nb)]
    m_in = bcum + per_row(m0)
    a_col = ig - bcum
    a_row = _column_to_row(a_col, R)
    r_i = lax.broadcasted_iota(jnp.int32, (R, R), 0)
    c_i = lax.broadcasted_iota(jnp.int32, (R, R), 1)
    same_seq = lax.shift_right_logical(r_i, log_steps) == lax.shift_right_logical(c_i, log_steps)
    d = jnp.where(same_seq, jnp.where(r_i >= c_i, bcum + a_row, -jnp.inf), -jnp.inf)
    m_t = jnp.maximum(m_in, jnp.max(d, axis=1, keepdims=True))
    p = jnp.exp(d - m_t)

    qb = q_ref[...]
    kb = k_ref[...]
    vb = v_ref[...]
    s = lax.dot_general(qb, kb, (((1,), (1,)), ((), ())), preferred_element_type=F32) * p
    w_prev = jnp.exp(m_in - m_t)
    inter = jnp.zeros((R, DV), F32)
    n_rows = jnp.zeros((R, DK), F32)
    for b in range(nb):
        qc = jnp.dot(qb, c0_ref[b].astype(BF16), preferred_element_type=F32)
        inter = jnp.where(seq_of_row == b, qc, inter)
        n_rows = jnp.where(seq_of_row == b, n0_ref[b], n_rows)
    num = jnp.dot(s.astype(BF16), vb, preferred_element_type=F32) + w_prev * inter
    den = (jnp.sum(s, axis=1, keepdims=True)
           + w_prev * jnp.sum(qb.astype(F32) * n_rows, axis=1, keepdims=True))
    hh = num / jnp.maximum(jnp.abs(den), jnp.exp(-m_t))
    h_out[...] = _head_out(hh, gh_ref[...], o_ref[...])

    last = [b * steps + steps - 1 for b in range(nb)]
    m_new = [m_t[r:r + 1, :] for r in last]
    b_last = [bcum[r:r + 1, :] for r in last]
    w_end = jnp.exp(per_row(b_last) + a_col - per_row(m_new))
    kw = kb.astype(F32) * w_end
    for b in range(nb):
        decay = jnp.exp(b_last[b] + m0[b] - m_new[b])
        kw_b = jnp.where(seq_of_row == b, kw, 0.0)
        c_out[b] = decay * c0_ref[b] + lax.dot_general(
            kw_b.astype(BF16), vb, (((0,), (0,)), ((), ())), preferred_element_type=F32)
        n_out[b] = decay * n0_ref[b] + jnp.sum(kw_b, axis=0, keepdims=True)
        m_out[b] = jnp.broadcast_to(m_new[b], (1, LANES))


def _mlstm_sample(q, k, v, o, gates, gate_bias, g_head, c0, n0, m0, row0, steps):
    nbs = c0.shape[0]
    nb = BF16_ROWS // steps
    R = nb * steps
    assert steps & (steps - 1) == 0 and nb * steps == BF16_ROWS and row0 % R == 0 and nbs % nb == 0
    r0 = row0 // R
    rows = lambda i, h: (r0 + i, h)
    out_rows = lambda i, h: (i, h)
    state = lambda i, h: (i, h, 0, 0)
    return pl.pallas_call(
        functools.partial(_mlstm_sample_kernel, nb=nb, steps=steps),
        grid=(nbs // nb, N_HEADS),
        in_specs=[pl.BlockSpec((R, DK), rows), pl.BlockSpec((R, DK), rows),
                  pl.BlockSpec((R, DV), rows), pl.BlockSpec((R, DV), rows),
                  pl.BlockSpec((R, LANES), lambda i, h: (r0 + i, 0)),
                  pl.BlockSpec((1, LANES), lambda i, h: (0, 0)),
                  pl.BlockSpec((1, DV), lambda i, h: (0, h)),
                  pl.BlockSpec((nb, None, DK, DV), state),
                  pl.BlockSpec((nb, None, 1, DK), state),
                  pl.BlockSpec((nb, None, 1, LANES), state)],
        out_specs=[pl.BlockSpec((R, DV), out_rows),
                   pl.BlockSpec((nb, None, DK, DV), state),
                   pl.BlockSpec((nb, None, 1, DK), state),
                   pl.BlockSpec((nb, None, 1, LANES), state)],
        out_shape=[jax.ShapeDtypeStruct((nbs * steps, N_HEADS * DV), BF16),
                   jax.ShapeDtypeStruct((nbs, N_HEADS, DK, DV), F32),
                   jax.ShapeDtypeStruct((nbs, N_HEADS, 1, DK), F32),
                   jax.ShapeDtypeStruct((nbs, N_HEADS, 1, LANES), F32)],
        compiler_params=_params("arbitrary", "arbitrary"),
        name="mlstm_sample",
    )(q, k, v, o, gates, gate_bias, g_head, c0, n0, m0)


def _gmlp_kernel(u_ref, v_ref, gln_ref, bln_ref, wsp_ref, bsp_ref, coef_ref, bias_ref,
                 out_ref, vg_ref, *, n_prompt_tiles, sample_len):
    x = v_ref[...]
    mu = jnp.mean(x, axis=-1, keepdims=True)
    xc = x - mu
    var = jnp.mean(xc * xc, axis=-1, keepdims=True)
    vg = xc * lax.rsqrt(var + EPS) * gln_ref[...] + bln_ref[...]
    u = u_ref[...]
    rows, width = x.shape
    gw = width // N_GROUPS
    is_prompt = pl.program_id(0) < n_prompt_tiles

    @pl.when(is_prompt)
    def _():
        r_i = lax.broadcasted_iota(jnp.int32, (rows, rows), 0)
        c_i = lax.broadcasted_iota(jnp.int32, (rows, rows), 1)
        for g in range(N_GROUPS):
            w = jnp.where(r_i >= c_i, wsp_ref[g], 0.0).astype(BF16)
            sl = slice(g * gw, (g + 1) * gw)
            mixed = jnp.dot(w, vg[:, sl].astype(BF16), preferred_element_type=F32) + bsp_ref[:, g:g + 1]
            out_ref[:, sl] = (u[:, sl] * mixed).astype(out_ref.dtype)

    @pl.when(jnp.logical_not(is_prompt))
    def _():
        vg_ref[...] = vg
        for g in range(N_GROUPS):
            sl = slice(g * gw, (g + 1) * gw)
            vgg = vg[:, sl]
            acc = bias_ref[:, g:g + 1] + coef_ref[:, g * sample_len:g * sample_len + 1] * vgg
            for j in range(1, sample_len):
                cj = coef_ref[:, g * sample_len + j:g * sample_len + j + 1]
                acc = acc + cj * pltpu.roll(vgg, j, axis=0)
            out_ref[:, sl] = (u[:, sl] * acc).astype(out_ref.dtype)


def _gmlp(uv, g_ln, b_ln, w_sp, b_sp_t, coef, bias, n_prompt_rows, n_sample_rows, sample_len):
    width = g_ln.shape[1]
    tr = GMLP_CHUNK
    n_pt = n_prompt_rows // tr
    n_st = n_sample_rows // tr
    kern = functools.partial(_gmlp_kernel, n_prompt_tiles=n_pt, sample_len=sample_len)
    const2 = lambda i: (0, 0)
    return pl.pallas_call(
        kern,
        grid=(n_pt + n_st,),
        in_specs=[pl.BlockSpec((tr, width), lambda i: (i, 0)),
                  pl.BlockSpec((tr, width), lambda i: (i, 1)),
                  pl.BlockSpec((1, width), const2), pl.BlockSpec((1, width), const2),
                  pl.BlockSpec((N_GROUPS, tr, tr), lambda i: (0, 0, 0)),
                  pl.BlockSpec((tr, N_GROUPS), const2),
                  pl.BlockSpec((tr, N_GROUPS * sample_len), const2),
                  pl.BlockSpec((tr, N_GROUPS), const2)],
        out_specs=[pl.BlockSpec((tr, width), lambda i: (i, 0)),
                   pl.BlockSpec((tr, width), lambda i: (jnp.maximum(i - n_pt, 0), 0))],
        out_shape=[jax.ShapeDtypeStruct((n_prompt_rows + n_sample_rows, width), BF16),
                   jax.ShapeDtypeStruct((n_sample_rows, width), F32)],
        compiler_params=_params("arbitrary"),
        name="gmlp",
    )(uv, uv, g_ln, b_ln, w_sp, b_sp_t, coef, bias)


def kernel(x_prompt, x_sample, p_prompt, p_sample, state_mlstm_conv, state_mlstm_C, state_mlstm_n, state_mlstm_m, g_ffn1_pre, w_ffn1_gate, w_ffn1_up, w_ffn1_down, g_ffn1_post, g_mix_pre, w_in, w_conv, b_conv, b_igate, b_fgate, g_head, w_a_out, g_ln_v, b_ln_v, w_spatial, b_spatial, w_b_out, w_o, g_mix_post, g_ffn2_pre, w_ffn2_gate, w_ffn2_up, w_ffn2_down, g_ffn2_post, g_ple_pre, w_ple_gate, w_ple_up, g_ple_post):
    assert w_in.shape[0] == 1, "single layer"
    nbp, seq, d = x_prompt.shape
    nbs, sseq, _ = x_sample.shape
    mp, ms = nbp * seq, nbs * sseq
    qk_w = 2 * N_HEADS * DK
    v_w = N_HEADS * DV
    d_b = g_ln_v.shape[1]
    gates_off = qk_w + 2 * v_w
    tail_off = gates_off + 2 * N_HEADS

    x_p, x_s = x_prompt.reshape(mp, d), x_sample.reshape(ms, d)
    p_all = jnp.concatenate([p_prompt[0].reshape(mp, -1), p_sample[0].reshape(ms, -1)], axis=0)

    xn1 = _rms_cast(x_p, x_s, g_ffn1_pre)
    hid1, wd1 = _ffn_up(xn1, w_ffn1_gate[0], w_ffn1_up[0], w_ffn1_down[0])
    h1, xn2 = _ffn_down(hid1, wd1, (x_p, x_s), g_ffn1_post, g_mix_pre)

    wt = jnp.swapaxes(w_in, 1, 2)[0]
    qk_pre, wa = _mm_nt(xn2, wt, 0, qk_w, _identity, F32, side=w_a_out[0])
    v_a, wb = _mm_nt(xn2, wt, qk_w, v_w, _identity, BF16, side=w_b_out[0])
    o_sig, wo = _mm_nt(xn2, wt, qk_w + v_w, v_w, _sigmoid, F32, side=w_o[0])
    gates, w_pu = _mm_nt(xn2, wt, gates_off, LANES, _identity, F32, side=w_ple_up[0])
    uv_gelu, w_pg = _mm_nt(xn2, wt, tail_off, 2 * d_b, _gelu, F32, side=w_ple_gate[0])
    ab_sig = _mm_nt(xn2, wt, tail_off + 2 * d_b, 2 * d, _sigmoid, F32)
    gate_bias = jnp.pad(jnp.concatenate([b_igate[0], b_fgate[0]]), (0, LANES - 2 * N_HEADS)).reshape(1, LANES)

    buf = jnp.pad(state_mlstm_conv[0], ((0, 0), (0, sseq - (CONV_W - 1)), (0, 0))).reshape(ms, qk_w)
    q_all, k_all = _conv(qk_pre, buf, w_conv[0], b_conv, mp, seq, sseq)
    ha_p, c_p, n_p, m_p = _mlstm_prompt(q_all, k_all, v_a, o_sig, gates, gate_bias, g_head, nbp, seq)
    init = (state_mlstm_C[0], state_mlstm_n[0].reshape(nbs, N_HEADS, 1, DK),
            jnp.broadcast_to(state_mlstm_m[0][:, :, None, None], (nbs, N_HEADS, 1, LANES)))
    ha_s, c_s, n_s, m_s = _mlstm_sample(q_all, k_all, v_a, o_sig, gates, gate_bias, g_head, *init, mp, sseq)

    t_idx = jnp.arange(GMLP_CHUNK) % sseq
    src = t_idx[:, None] - jnp.arange(sseq)[None, :]
    w_small = w_spatial[0][:, :sseq, :sseq]
    coef = jnp.where(src[None] >= 0, w_small[:, t_idx[:, None], jnp.maximum(src, 0)], 0.0)
    coef = coef.transpose(1, 0, 2).reshape(GMLP_CHUNK, N_GROUPS * sseq)
    bias_s = b_spatial[0][:, t_idx].T
    sg_all, vg_s = _gmlp(uv_gelu, g_ln_v, b_ln_v, w_spatial[0], b_spatial[0].T, coef, bias_s, mp, ms, sseq)

    h2, xn3 = _merge(ha_p, ha_s, sg_all, ab_sig, h1, wa, wb, wo, g_mix_post, g_ffn2_pre)
    hid2, wd2 = _ffn_up(xn3, w_ffn2_gate[0], w_ffn2_up[0], w_ffn2_down[0])
    h3, xn4 = _ffn_down(hid2, wd2, (h2,), g_ffn2_post, g_ple_pre)
    y_p, y_s = _ple(xn4, p_all, h3, w_pg, w_pu, g_ple_post, mp)

    keep = CONV_W - 1
    conv_prompt = jnp.stack([qk_pre[(b + 1) * seq - keep:(b + 1) * seq] for b in range(nbp)])
    conv_sample = qk_pre[mp:].reshape(nbs, sseq, qk_w)[:, sseq - keep:]
    return (y_p.reshape(nbp, seq, d), y_s.reshape(nbs, sseq, d),
            conv_prompt[None], c_p[None], n_p.reshape(1, nbp, N_HEADS, DK), m_p[:, :, 0, 0][None],
            conv_sample[None], c_s[None], n_s.reshape(1, nbs, N_HEADS, DK), m_s[:, :, 0, 0][None],
            vg_s.reshape(1, nbs, sseq, d_b))
```

```python
import functools

import jax
import jax.numpy as jnp
from jax import lax
from jax.experimental import pallas as pl
from jax.experimental.pallas import tpu as pltpu

F32 = jnp.float32
BF16 = jnp.bfloat16
EPS = 1e-6

N_HEADS = 4
DK = 256
DV = 512
CONV_W = 4
N_GROUPS = 4
GMLP_CHUNK = 128
LANES = 128
SUBLANES = 8
BF16_ROWS = 16
MLSTM_CHUNK = 256
VMEM_LIMIT = 54 * 1024 * 1024


def _params(*sem):
    return pltpu.CompilerParams(dimension_semantics=sem, vmem_limit_bytes=VMEM_LIMIT)


def _rms(x, g):
    return x * lax.rsqrt(jnp.mean(x * x, axis=-1, keepdims=True) + EPS) * g


def _gelu(x):
    return 0.5 * x * (1.0 + lax.erf(x * 0.7071067811865476))


def _sigmoid(x):
    return jax.nn.sigmoid(x)


def _identity(x):
    return x


def _log_sigmoid(x):
    return jnp.minimum(x, 0.0) - jnp.log1p(jnp.exp(-jnp.abs(x)))


def _resident(shape):
    return pl.BlockSpec(shape, lambda *_: (0,) * len(shape), pipeline_mode=pl.Buffered(1))


def _split_specs(tm, width, n_p):
    return [pl.BlockSpec((tm, width), lambda i: (jnp.minimum(i, n_p - 1), 0)),
            pl.BlockSpec((tm, width), lambda i: (jnp.maximum(i - n_p, 0), 0))]


def _head_gates(g, head):
    lane = lax.broadcasted_iota(jnp.int32, g.shape, 1)
    ig = jnp.sum(jnp.where(lane == head, g, 0.0), axis=1, keepdims=True)
    fg = jnp.sum(jnp.where(lane == head + N_HEADS, g, 0.0), axis=1, keepdims=True)
    return ig, fg


def _column_to_row(col, n):
    r_i = lax.broadcasted_iota(jnp.int32, (n, n), 0)
    c_i = lax.broadcasted_iota(jnp.int32, (n, n), 1)
    return jnp.sum(jnp.where(r_i == c_i, jnp.broadcast_to(col, (n, n)), 0.0), axis=0, keepdims=True)


def _rms_cast_kernel(xp_ref, xs_ref, g_ref, o_ref, *, n_p):
    i = pl.program_id(0)

    @pl.when(i < n_p)
    def _():
        o_ref[...] = _rms(xp_ref[...], g_ref[...]).astype(o_ref.dtype)

    @pl.when(i >= n_p)
    def _():
        o_ref[...] = _rms(xs_ref[...], g_ref[...]).astype(o_ref.dtype)


def _rms_cast(x_p, x_s, g, tm=512):
    (mp, d), ms = x_p.shape, x_s.shape[0]
    n_p = mp // tm
    return pl.pallas_call(
        functools.partial(_rms_cast_kernel, n_p=n_p),
        grid=((mp + ms) // tm,),
        in_specs=_split_specs(tm, d, n_p) + [pl.BlockSpec((1, d), lambda i: (0, 0))],
        out_specs=pl.BlockSpec((tm, d), lambda i: (i, 0)),
        out_shape=jax.ShapeDtypeStruct((mp + ms, d), BF16),
        compiler_params=_params("arbitrary"),
        name="rms_cast",
    )(x_p, x_s, g)


def _side_specs(side, n_steps, n_inner):
    rows, cols = side.shape
    slab = rows // n_steps
    assert slab * n_steps == rows and slab % BF16_ROWS == 0
    spec = pl.BlockSpec((slab, cols), lambda j, i: (j * n_inner + i, 0))
    return spec, jax.ShapeDtypeStruct((rows, cols), BF16)


def _ffn_up_kernel(x_ref, wg_ref, wu_ref, side_ref, o_ref, side_out, wgb, wub):
    @pl.when(pl.program_id(1) == 0)
    def _():
        wgb[...] = wg_ref[...].astype(BF16)
        wub[...] = wu_ref[...].astype(BF16)

    side_out[...] = side_ref[...].astype(BF16)
    x = x_ref[...]
    g = jnp.dot(x, wgb[...], preferred_element_type=F32)
    u = jnp.dot(x, wub[...], preferred_element_type=F32)
    o_ref[...] = (g * _sigmoid(g) * u).astype(o_ref.dtype)


def _ffn_up(xn, wg, wu, side, tm=1088, tf=512):
    m, d = xn.shape
    f = wg.shape[1]
    grid = (f // tf, m // tm)
    side_spec, side_shape = _side_specs(side, grid[0] * grid[1], grid[1])
    return pl.pallas_call(
        _ffn_up_kernel,
        grid=grid,
        in_specs=[pl.BlockSpec((tm, d), lambda j, i: (i, 0)),
                  pl.BlockSpec((d, tf), lambda j, i: (0, j)),
                  pl.BlockSpec((d, tf), lambda j, i: (0, j)),
                  side_spec],
        out_specs=[pl.BlockSpec((tm, tf), lambda j, i: (i, j)), side_spec],
        out_shape=[jax.ShapeDtypeStruct((m, f), BF16), side_shape],
        scratch_shapes=[pltpu.VMEM((d, tf), BF16), pltpu.VMEM((d, tf), BF16)],
        compiler_params=_params("arbitrary", "arbitrary"),
        name="ffn_up",
    )(xn, wg, wu, side)


def _mm_nt_kernel(*refs, act, n_side):
    x_ref, w_ref = refs[:2]
    o_ref = refs[2 + n_side]
    wb = refs[-1]
    for side_ref, side_out in zip(refs[2:2 + n_side], refs[3 + n_side:3 + 2 * n_side]):
        side_out[...] = side_ref[...].astype(BF16)

    @pl.when(pl.program_id(1) == 0)
    def _():
        wb[...] = w_ref[...].astype(BF16)

    y = lax.dot_general(x_ref[...], wb[...], (((1,), (1,)), ((), ())), preferred_element_type=F32)
    o_ref[...] = act(y).astype(o_ref.dtype)


def _mm_nt(x, wt, row_off, n_out, act, out_dtype, sides=(), tm=1088, tn=1024):
    m, k = x.shape
    tn = min(tn, n_out)
    grid = (n_out // tn, m // tm)
    assert grid[0] * tn == n_out and row_off % SUBLANES == 0
    if row_off % tn == 0:
        w_spec = pl.BlockSpec((tn, k), lambda j, i: (row_off // tn + j, 0))
    else:
        w_spec = pl.BlockSpec((pl.Element(tn), pl.Element(k)),
                              lambda j, i: (pl.multiple_of(row_off + j * tn, SUBLANES), 0))
    in_specs = [pl.BlockSpec((tm, k), lambda j, i: (i, 0)), w_spec]
    out_specs = [pl.BlockSpec((tm, tn), lambda j, i: (i, j))]
    out_shape = [jax.ShapeDtypeStruct((m, n_out), out_dtype)]
    args = [x, wt]
    for side in sides:
        side_spec, side_shape = _side_specs(side, grid[0] * grid[1], grid[1])
        in_specs.append(side_spec)
        out_specs.append(side_spec)
        out_shape.append(side_shape)
        args.append(side)
    out = pl.pallas_call(
        functools.partial(_mm_nt_kernel, act=act, n_side=len(sides)),
        grid=grid,
        in_specs=in_specs,
        out_specs=out_specs,
        out_shape=out_shape,
        scratch_shapes=[pltpu.VMEM((tn, k), BF16)],
        compiler_params=_params("arbitrary", "arbitrary"),
        name="mm_nt",
    )(*args)
    return out if sides else out[0]


def _ffn_down_kernel(*refs, n_p):
    if n_p is None:
        h_ref, wd_ref, x_ref, gpost_ref, gnext_ref, hout_ref, xn_ref = refs
        x = x_ref[...]
    else:
        h_ref, wd_ref, xp_ref, xs_ref, gpost_ref, gnext_ref, hout_ref, xn_ref = refs
        x = jnp.where(pl.program_id(0) < n_p, xp_ref[...], xs_ref[...])
    y = jnp.dot(h_ref[...], wd_ref[...], preferred_element_type=F32)
    h = x + 0.5 * _rms(y, gpost_ref[...])
    hout_ref[...] = h
    xn_ref[...] = _rms(h, gnext_ref[...]).astype(xn_ref.dtype)


def _ffn_down(hid, wd, x_parts, g_post, g_next, tm=256):
    m, f = hid.shape
    d = wd.shape[1]
    row = lambda i: (i, 0)
    if len(x_parts) == 1:
        n_p, x_specs = None, [pl.BlockSpec((tm, d), row)]
    else:
        n_p = x_parts[0].shape[0] // tm
        x_specs = _split_specs(tm, d, n_p)
    return pl.pallas_call(
        functools.partial(_ffn_down_kernel, n_p=n_p),
        grid=(m // tm,),
        in_specs=[pl.BlockSpec((tm, f), row), _resident((f, d))] + x_specs
                 + [_resident((1, d)), _resident((1, d))],
        out_specs=[pl.BlockSpec((tm, d), row), pl.BlockSpec((tm, d), row)],
        out_shape=[jax.ShapeDtypeStruct((m, d), F32), jax.ShapeDtypeStruct((m, d), BF16)],
        compiler_params=_params("arbitrary"),
        name="ffn_down",
    )(hid, wd, *x_parts, g_post, g_next)


def _merge_kernel(hap_ref, has_ref, sg_ref, gates_a_ref, gates_b_ref, h_ref, wa_ref, wb_ref, wo_ref,
                  gpost_ref, gnext_ref, hout_ref, xn_ref, *, n_p):
    ha = jnp.where(pl.program_id(0) < n_p, hap_ref[...], has_ref[...])
    ya = jnp.dot(ha, wa_ref[...], preferred_element_type=F32)
    yb = jnp.dot(sg_ref[...], wb_ref[...], preferred_element_type=F32)
    mixin = (gates_a_ref[...] * ya + gates_b_ref[...] * yb).astype(BF16)
    mix = jnp.dot(mixin, wo_ref[...], preferred_element_type=F32)
    h = h_ref[...] + _rms(mix, gpost_ref[...])
    hout_ref[...] = h
    xn_ref[...] = _rms(h, gnext_ref[...]).astype(xn_ref.dtype)


def _merge(ha_p, ha_s, sg, ab_sig, h, wa, wb, wo, g_post, g_next, tm=256):
    m, d = h.shape
    n_p = ha_p.shape[0] // tm
    row = lambda i: (i, 0)
    return pl.pallas_call(
        functools.partial(_merge_kernel, n_p=n_p),
        grid=(m // tm,),
        in_specs=_split_specs(tm, d, n_p)
                 + [pl.BlockSpec((tm, d), row),
                    pl.BlockSpec((tm, d), lambda i: (i, 0)), pl.BlockSpec((tm, d), lambda i: (i, 1)),
                    pl.BlockSpec((tm, d), row),
                    _resident((d, d)), _resident((d, d)), _resident((d, d)),
                    _resident((1, d)), _resident((1, d))],
        out_specs=[pl.BlockSpec((tm, d), row), pl.BlockSpec((tm, d), row)],
        out_shape=[jax.ShapeDtypeStruct((m, d), F32), jax.ShapeDtypeStruct((m, d), BF16)],
        compiler_params=_params("arbitrary"),
        name="merge",
    )(ha_p, ha_s, sg, ab_sig, ab_sig, h, wa, wb, wo, g_post, g_next)


def _ple_kernel(xn_ref, p_ref, h_ref, wg_ref, wu_ref, gpost_ref, outp_ref, outs_ref, *, n_p):
    gate = _sigmoid(jnp.dot(xn_ref[...], wg_ref[...], preferred_element_type=F32))
    up = jnp.dot(p_ref[...].astype(BF16), wu_ref[...], preferred_element_type=F32)
    out = h_ref[...] + _rms(gate * up, gpost_ref[...])
    i = pl.program_id(0)

    @pl.when(i < n_p)
    def _():
        outp_ref[...] = out

    @pl.when(i >= n_p)
    def _():
        outs_ref[...] = out


def _ple(xn, p, h, wg, wu, g_post, mp, tm=256):
    m, d = h.shape
    dp = p.shape[1]
    n_p = mp // tm
    row = lambda i: (i, 0)
    return pl.pallas_call(
        functools.partial(_ple_kernel, n_p=n_p),
        grid=(m // tm,),
        in_specs=[pl.BlockSpec((tm, d), row), pl.BlockSpec((tm, dp), row), pl.BlockSpec((tm, d), row),
                  _resident((d, d)), _resident((dp, d)), _resident((1, d))],
        out_specs=_split_specs(tm, d, n_p),
        out_shape=[jax.ShapeDtypeStruct((mp, d), F32), jax.ShapeDtypeStruct((m - mp, d), F32)],
        compiler_params=_params("arbitrary"),
        name="ple",
    )(xn, p, h, wg, wu, g_post)


def _conv_kernel(x_ref, buf_ref, w_ref, b_ref, q_ref, k_ref, prev, *, ts, n_p, tiles_per_seq, sample_len):
    i = pl.program_id(0)
    w = w_ref[...]
    c = x_ref.shape[1]

    def finish(y):
        s = y * _sigmoid(y)
        q_ref[...] = s[:, :c // 2].astype(q_ref.dtype)
        k_ref[...] = (s[:, c // 2:] * (DK ** -0.5)).astype(k_ref.dtype)

    @pl.when(i < n_p)
    def _():
        @pl.when(lax.rem(i, tiles_per_seq) == 0)
        def _():
            prev[...] = jnp.zeros_like(prev)

        x = x_ref[...]
        tail = prev[...]
        y = b_ref[...] + x * w[CONV_W - 1:CONV_W, :]
        rowid = lax.broadcasted_iota(jnp.int32, (SUBLANES, c), 0)
        for j in range(1, CONV_W):
            xr = pltpu.roll(x, j, axis=0)
            tr = pltpu.roll(tail, j, axis=0)
            top = jnp.where(rowid < j, tr, xr[0:SUBLANES])
            xs = jnp.concatenate([top, xr[SUBLANES:]], axis=0)
            y = y + xs * w[CONV_W - 1 - j:CONV_W - j, :]
        prev[...] = x[ts - SUBLANES:ts]
        finish(y)

    @pl.when(i >= n_p)
    def _():
        x = x_ref[...]
        buf = buf_ref[...]
        t = jnp.bitwise_and(lax.broadcasted_iota(jnp.int32, (ts, c), 0), sample_len - 1)
        y = b_ref[...] + x * w[CONV_W - 1:CONV_W, :]
        for j in range(1, CONV_W):
            back = (j - (CONV_W - 1)) % ts
            from_buf = buf if back == 0 else pltpu.roll(buf, back, axis=0)
            xs = jnp.where(t >= j, pltpu.roll(x, j, axis=0), from_buf)
            y = y + xs * w[CONV_W - 1 - j:CONV_W - j, :]
        finish(y)


def _conv(x, buf, w, b, mp, seq, sample_len, ts=256):
    m, c = x.shape
    n_p = mp // ts
    assert sample_len & (sample_len - 1) == 0 and sample_len >= CONV_W - 1 and ts % sample_len == 0
    kern = functools.partial(_conv_kernel, ts=ts, n_p=n_p, tiles_per_seq=seq // ts, sample_len=sample_len)
    row = lambda i: (i, 0)
    return pl.pallas_call(
        kern,
        grid=(m // ts,),
        in_specs=[pl.BlockSpec((ts, c), row),
                  pl.BlockSpec((ts, c), lambda i: (jnp.maximum(i - n_p, 0), 0)),
                  pl.BlockSpec((CONV_W, c), lambda i: (0, 0)),
                  pl.BlockSpec((1, c), lambda i: (0, 0))],
        out_specs=[pl.BlockSpec((ts, c // 2), row), pl.BlockSpec((ts, c // 2), row)],
        out_shape=[jax.ShapeDtypeStruct((m, c // 2), BF16), jax.ShapeDtypeStruct((m, c // 2), BF16)],
        scratch_shapes=[pltpu.VMEM((SUBLANES, c), F32)],
        compiler_params=_params("arbitrary"),
        name="conv",
    )(x, buf, w, b)


def _head_out(hh, gh, o):
    hn = hh * lax.rsqrt(jnp.mean(hh * hh, axis=1, keepdims=True) + EPS) * gh
    return (o * hn).astype(BF16)


def _mlstm_prompt_kernel(q_ref, k_ref, v_ref, o_ref, g_ref, gb_ref, gh_ref,
                         h_out, c_out, n_out, m_out, c_s, n_s, m_s, *, chunk):
    L = chunk
    head = pl.program_id(1)
    ci = pl.program_id(2)

    @pl.when(ci == 0)
    def _():
        c_s[...] = jnp.zeros_like(c_s)
        n_s[...] = jnp.zeros_like(n_s)
        m_s[...] = jnp.zeros_like(m_s)

    ig, fg = _head_gates(g_ref[...] + gb_ref[...], head)
    lf = _log_sigmoid(fg)

    bc = jnp.broadcast_to(lf, (L, LANES))
    rowi = lax.broadcasted_iota(jnp.int32, (L, LANES), 0)
    sh = 1
    while sh < L:
        bc = bc + jnp.where(rowi >= sh, pltpu.roll(bc, sh, axis=0), 0.0)
        sh *= 2
    bcum = bc[:, 0:1]

    m_prev = m_s[0:1, 0:1]
    m_in = bcum + m_prev
    a_col = ig - bcum
    a_row = _column_to_row(a_col, L)
    r_i = lax.broadcasted_iota(jnp.int32, (L, L), 0)
    c_i = lax.broadcasted_iota(jnp.int32, (L, L), 1)
    d = jnp.where(r_i >= c_i, bcum + a_row, -jnp.inf)
    m_t = jnp.maximum(m_in, jnp.max(d, axis=1, keepdims=True))
    p = jnp.exp(d - m_t)

    qb = q_ref[...]
    kb = k_ref[...]
    vb = v_ref[...]
    s = lax.dot_general(qb, kb, (((1,), (1,)), ((), ())), preferred_element_type=F32) * p
    w_prev = jnp.exp(m_in - m_t)
    cb = c_s[...].astype(BF16)
    num = (jnp.dot(s.astype(BF16), vb, preferred_element_type=F32)
           + w_prev * jnp.dot(qb, cb, preferred_element_type=F32))
    den = (jnp.sum(s, axis=1, keepdims=True)
           + w_prev * jnp.sum(qb.astype(F32) * n_s[...], axis=1, keepdims=True))
    hh = num / jnp.maximum(jnp.abs(den), jnp.exp(-m_t))
    h_out[...] = _head_out(hh, gh_ref[...], o_ref[...])

    m_new = m_t[L - 1:L, :]
    b_last = bcum[L - 1:L, :]
    w_end = jnp.exp(b_last + a_col - m_new)
    decay = jnp.exp(b_last + m_prev - m_new)
    kw = kb.astype(F32) * w_end
    c_s[...] = decay * c_s[...] + lax.dot_general(
        kw.astype(BF16), vb, (((0,), (0,)), ((), ())), preferred_element_type=F32)
    n_s[...] = decay * n_s[...] + jnp.sum(kw, axis=0, keepdims=True)
    m_s[...] = jnp.broadcast_to(m_new, m_s.shape)

    @pl.when(ci == pl.num_programs(2) - 1)
    def _():
        c_out[...] = c_s[...]
        n_out[...] = n_s[...]
        m_out[...] = m_s[...]


def _mlstm_prompt(q, k, v, o, gates, gate_bias, g_head, nb, seq, chunk=MLSTM_CHUNK):
    L = chunk
    nc = seq // L
    rows = lambda b, h, c: (b * nc + c, h)
    state = lambda b, h, c: (b, h, 0, 0)
    return pl.pallas_call(
        functools.partial(_mlstm_prompt_kernel, chunk=L),
        grid=(nb, N_HEADS, nc),
        in_specs=[pl.BlockSpec((L, DK), rows), pl.BlockSpec((L, DK), rows),
                  pl.BlockSpec((L, DV), rows), pl.BlockSpec((L, DV), rows),
                  pl.BlockSpec((L, LANES), lambda b, h, c: (b * nc + c, 0)),
                  pl.BlockSpec((1, LANES), lambda b, h, c: (0, 0)),
                  pl.BlockSpec((1, DV), lambda b, h, c: (0, h))],
        out_specs=[pl.BlockSpec((L, DV), rows),
                   pl.BlockSpec((None, None, DK, DV), state),
                   pl.BlockSpec((None, None, 1, DK), state),
                   pl.BlockSpec((None, None, 1, LANES), state)],
        out_shape=[jax.ShapeDtypeStruct((nb * seq, N_HEADS * DV), BF16),
                   jax.ShapeDtypeStruct((nb, N_HEADS, DK, DV), F32),
                   jax.ShapeDtypeStruct((nb, N_HEADS, 1, DK), F32),
                   jax.ShapeDtypeStruct((nb, N_HEADS, 1, LANES), F32)],
        scratch_shapes=[pltpu.VMEM((DK, DV), F32), pltpu.VMEM((1, DK), F32),
                        pltpu.VMEM((1, LANES), F32)],
        compiler_params=_params("arbitrary", "arbitrary", "arbitrary"),
        name="mlstm_prompt",
    )(q, k, v, o, gates, gate_bias, g_head)


def _mlstm_sample_unit(head, q_ref, k_ref, v_ref, o_ref, g_ref, gb_ref, gh_ref, c0_ref, n0_ref, m0_ref,
                       h_out, c_out, n_out, m_out, *, nb, steps):
    R = nb * steps
    log_steps = steps.bit_length() - 1

    ig, fg = _head_gates(g_ref[...] + gb_ref[...], head)
    lf = _log_sigmoid(fg)

    bc = jnp.broadcast_to(lf, (R, LANES))
    tt = jnp.bitwise_and(lax.broadcasted_iota(jnp.int32, (R, LANES), 0), steps - 1)
    sh = 1
    while sh < steps:
        bc = bc + jnp.where(tt >= sh, pltpu.roll(bc, sh, axis=0), 0.0)
        sh *= 2
    bcum = bc[:, 0:1]

    seq_of_row = lax.shift_right_logical(lax.broadcasted_iota(jnp.int32, (R, 1), 0), log_steps)

    def per_row(vals):
        out = jnp.broadcast_to(vals[0], (R, 1))
        for b in range(1, nb):
            out = jnp.where(seq_of_row == b, vals[b], out)
        return out

    m0 = [m0_ref[b][0:1, 0:1] for b in range(nb)]
    m_in = bcum + per_row(m0)
    a_col = ig - bcum
    a_row = _column_to_row(a_col, R)
    r_i = lax.broadcasted_iota(jnp.int32, (R, R), 0)
    c_i = lax.broadcasted_iota(jnp.int32, (R, R), 1)
    same_seq = lax.shift_right_logical(r_i, log_steps) == lax.shift_right_logical(c_i, log_steps)
    d = jnp.where(same_seq, jnp.where(r_i >= c_i, bcum + a_row, -jnp.inf), -jnp.inf)
    m_t = jnp.maximum(m_in, jnp.max(d, axis=1, keepdims=True))
    p = jnp.exp(d - m_t)

    qb = q_ref[...]
    kb = k_ref[...]
    vb = v_ref[...]
    s = lax.dot_general(qb, kb, (((1,), (1,)), ((), ())), preferred_element_type=F32) * p
    w_prev = jnp.exp(m_in - m_t)
    inter = jnp.zeros((R, DV), F32)
    n_rows = jnp.zeros((R, DK), F32)
    for b in range(nb):
        qc = jnp.dot(qb, c0_ref[b].astype(BF16), preferred_element_type=F32)
        inter = jnp.where(seq_of_row == b, qc, inter)
        n_rows = jnp.where(seq_of_row == b, n0_ref[b], n_rows)
    num = jnp.dot(s.astype(BF16), vb, preferred_element_type=F32) + w_prev * inter
    den = (jnp.sum(s, axis=1, keepdims=True)
           + w_prev * jnp.sum(qb.astype(F32) * n_rows, axis=1, keepdims=True))
    hh = num / jnp.maximum(jnp.abs(den), jnp.exp(-m_t))
    h_out[...] = _head_out(hh, gh_ref[...], o_ref[...])

    last = [b * steps + steps - 1 for b in range(nb)]
    m_new = [m_t[r:r + 1, :] for r in last]
    b_last = [bcum[r:r + 1, :] for r in last]
    w_end = jnp.exp(per_row(b_last) + a_col - per_row(m_new))
    kw = kb.astype(F32) * w_end
    for b in range(nb):
        decay = jnp.exp(b_last[b] + m0[b] - m_new[b])
        kw_b = jnp.where(seq_of_row == b, kw, 0.0)
        c_out[b] = decay * c0_ref[b] + lax.dot_general(
            kw_b.astype(BF16), vb, (((0,), (0,)), ((), ())), preferred_element_type=F32)
        n_out[b] = decay * n0_ref[b] + jnp.sum(kw_b, axis=0, keepdims=True)
        m_out[b] = jnp.broadcast_to(m_new[b], (1, LANES))


def _mm_nt_mlstm_kernel(*refs, act, unit0, n_inner, nb, steps, n_prev):
    x_ref, w_ref = refs[:2]
    unit_in = refs[2:12]
    o_ref = refs[12 + n_prev]
    unit_out = refs[13 + n_prev:17 + n_prev]
    wb = refs[17 + n_prev]

    @pl.when(pl.program_id(1) == 0)
    def _():
        wb[...] = w_ref[...].astype(BF16)

    unit = unit0 + pl.program_id(0) * n_inner + pl.program_id(1)
    _mlstm_sample_unit(lax.rem(unit, N_HEADS), *unit_in, *unit_out, nb=nb, steps=steps)
    y = lax.dot_general(x_ref[...], wb[...], (((1,), (1,)), ((), ())), preferred_element_type=F32)
    o_ref[...] = act(y).astype(o_ref.dtype)


def _mm_nt_mlstm(x, wt, row_off, n_out, act, out_dtype, sample, unit0, prev=None, tm=1088, tn=512):
    *unit_args, row0, steps = sample
    m, k = x.shape
    grid = (n_out // tn, m // tm)
    nbs = unit_args[7].shape[0]
    nb = BF16_ROWS // steps
    R = nb * steps
    assert grid[0] * tn == n_out and row_off % SUBLANES == 0
    assert steps & (steps - 1) == 0 and nb * steps == BF16_ROWS and row0 % R == 0 and nbs % nb == 0
    assert 0 <= unit0 and unit0 + grid[0] * grid[1] <= (nbs // nb) * N_HEADS
    r0 = row0 // R

    def unit(j, i):
        u = unit0 + j * grid[1] + i
        return u // N_HEADS, lax.rem(u, N_HEADS)

    def rows(j, i):
        g, h = unit(j, i)
        return r0 + g, h

    def state(j, i):
        g, h = unit(j, i)
        return g, h, 0, 0

    in_specs = [pl.BlockSpec((tm, k), lambda j, i: (i, 0)),
                pl.BlockSpec((pl.Element(tn), pl.Element(k)),
                             lambda j, i: (pl.multiple_of(row_off + j * tn, SUBLANES), 0)),
                pl.BlockSpec((R, DK), rows), pl.BlockSpec((R, DK), rows),
                pl.BlockSpec((R, DV), rows), pl.BlockSpec((R, DV), rows),
                pl.BlockSpec((R, LANES), lambda j, i: (r0 + unit(j, i)[0], 0)),
                pl.BlockSpec((1, LANES), lambda j, i: (0, 0)),
                pl.BlockSpec((1, DV), lambda j, i: (0, unit(j, i)[1])),
                pl.BlockSpec((nb, None, DK, DV), state),
                pl.BlockSpec((nb, None, 1, DK), state),
                pl.BlockSpec((nb, None, 1, LANES), state)]
    args = [x, wt, *unit_args]
    aliases = {}
    if prev is not None:
        aliases = {len(args) + a: 1 + a for a in range(len(prev))}
        in_specs += [pl.BlockSpec(memory_space=pl.ANY)] * len(prev)
        args += list(prev)
    return pl.pallas_call(
        functools.partial(_mm_nt_mlstm_kernel, act=act, unit0=unit0, n_inner=grid[1], nb=nb, steps=steps,
                          n_prev=0 if prev is None else len(prev)),
        grid=grid,
        in_specs=in_specs,
        out_specs=[pl.BlockSpec((tm, tn), lambda j, i: (i, j)),
                   pl.BlockSpec((R, DV), lambda j, i: unit(j, i)),
                   pl.BlockSpec((nb, None, DK, DV), state),
                   pl.BlockSpec((nb, None, 1, DK), state),
                   pl.BlockSpec((nb, None, 1, LANES), state)],
        out_shape=[jax.ShapeDtypeStruct((m, n_out), out_dtype),
                   jax.ShapeDtypeStruct((nbs * steps, N_HEADS * DV), BF16),
                   jax.ShapeDtypeStruct((nbs, N_HEADS, DK, DV), F32),
                   jax.ShapeDtypeStruct((nbs, N_HEADS, 1, DK), F32),
                   jax.ShapeDtypeStruct((nbs, N_HEADS, 1, LANES), F32)],
        scratch_shapes=[pltpu.VMEM((tn, k), BF16)],
        input_output_aliases=aliases,
        compiler_params=_params("arbitrary", "arbitrary"),
        name="mm_nt_mlstm",
    )(*args)


def _gmlp_kernel(u_ref, v_ref, gln_ref, bln_ref, wsp_ref, bsp_ref, coef_ref, bias_ref,
                 out_ref, vg_ref, *, n_prompt_tiles, sample_len):
    x = v_ref[...]
    mu = jnp.mean(x, axis=-1, keepdims=True)
    xc = x - mu
    var = jnp.mean(xc * xc, axis=-1, keepdims=True)
    vg = xc * lax.rsqrt(var + EPS) * gln_ref[...] + bln_ref[...]
    u = u_ref[...]
    rows, width = x.shape
    gw = width // N_GROUPS
    is_prompt = pl.program_id(0) < n_prompt_tiles

    @pl.when(is_prompt)
    def _():
        r_i = lax.broadcasted_iota(jnp.int32, (rows, rows), 0)
        c_i = lax.broadcasted_iota(jnp.int32, (rows, rows), 1)
        for g in range(N_GROUPS):
            w = jnp.where(r_i >= c_i, wsp_ref[g], 0.0).astype(BF16)
            sl = slice(g * gw, (g + 1) * gw)
            mixed = jnp.dot(w, vg[:, sl].astype(BF16), preferred_element_type=F32) + bsp_ref[:, g:g + 1]
            out_ref[:, sl] = (u[:, sl] * mixed).astype(out_ref.dtype)

    @pl.when(jnp.logical_not(is_prompt))
    def _():
        vg_ref[...] = vg
        for g in range(N_GROUPS):
            sl = slice(g * gw, (g + 1) * gw)
            vgg = vg[:, sl]
            acc = bias_ref[:, g:g + 1] + coef_ref[:, g * sample_len:g * sample_len + 1] * vgg
            for j in range(1, sample_len):
                cj = coef_ref[:, g * sample_len + j:g * sample_len + j + 1]
                acc = acc + cj * pltpu.roll(vgg, j, axis=0)
            out_ref[:, sl] = (u[:, sl] * acc).astype(out_ref.dtype)


def _gmlp(uv, g_ln, b_ln, w_sp, b_sp_t, coef, bias, n_prompt_rows, n_sample_rows, sample_len):
    width = g_ln.shape[1]
    tr = GMLP_CHUNK
    n_pt = n_prompt_rows // tr
    n_st = n_sample_rows // tr
    kern = functools.partial(_gmlp_kernel, n_prompt_tiles=n_pt, sample_len=sample_len)
    const2 = lambda i: (0, 0)
    return pl.pallas_call(
        kern,
        grid=(n_pt + n_st,),
        in_specs=[pl.BlockSpec((tr, width), lambda i: (i, 0)),
                  pl.BlockSpec((tr, width), lambda i: (i, 1)),
                  pl.BlockSpec((1, width), const2), pl.BlockSpec((1, width), const2),
                  pl.BlockSpec((N_GROUPS, tr, tr), lambda i: (0, 0, 0)),
                  pl.BlockSpec((tr, N_GROUPS), const2),
                  pl.BlockSpec((tr, N_GROUPS * sample_len), const2),
                  pl.BlockSpec((tr, N_GROUPS), const2)],
        out_specs=[pl.BlockSpec((tr, width), lambda i: (i, 0)),
                   pl.BlockSpec((tr, width), lambda i: (jnp.maximum(i - n_pt, 0), 0))],
        out_shape=[jax.ShapeDtypeStruct((n_prompt_rows + n_sample_rows, width), BF16),
                   jax.ShapeDtypeStruct((n_sample_rows, width), F32)],
        compiler_params=_params("arbitrary"),
        name="gmlp",
    )(uv, uv, g_ln, b_ln, w_sp, b_sp_t, coef, bias)


def kernel(x_prompt, x_sample, p_prompt, p_sample, state_mlstm_conv, state_mlstm_C, state_mlstm_n, state_mlstm_m, g_ffn1_pre, w_ffn1_gate, w_ffn1_up, w_ffn1_down, g_ffn1_post, g_mix_pre, w_in, w_conv, b_conv, b_igate, b_fgate, g_head, w_a_out, g_ln_v, b_ln_v, w_spatial, b_spatial, w_b_out, w_o, g_mix_post, g_ffn2_pre, w_ffn2_gate, w_ffn2_up, w_ffn2_down, g_ffn2_post, g_ple_pre, w_ple_gate, w_ple_up, g_ple_post):
    assert w_in.shape[0] == 1, "single layer"
    nbp, seq, d = x_prompt.shape
    nbs, sseq, _ = x_sample.shape
    mp, ms = nbp * seq, nbs * sseq
    qk_w = 2 * N_HEADS * DK
    v_w = N_HEADS * DV
    d_b = g_ln_v.shape[1]
    gates_off = qk_w + 2 * v_w
    tail_off = gates_off + 2 * N_HEADS

    x_p, x_s = x_prompt.reshape(mp, d), x_sample.reshape(ms, d)
    p_all = jnp.concatenate([p_prompt[0].reshape(mp, -1), p_sample[0].reshape(ms, -1)], axis=0)

    xn1 = _rms_cast(x_p, x_s, g_ffn1_pre)
    hid1, wd1 = _ffn_up(xn1, w_ffn1_gate[0], w_ffn1_up[0], w_ffn1_down[0])
    h1, xn2 = _ffn_down(hid1, wd1, (x_p, x_s), g_ffn1_post, g_mix_pre)

    wt = jnp.swapaxes(w_in, 1, 2)[0]
    qk_pre, wa = _mm_nt(xn2, wt, 0, qk_w, _identity, F32, sides=(w_a_out[0],))
    v_a, wb = _mm_nt(xn2, wt, qk_w, v_w, _identity, BF16, sides=(w_b_out[0],))
    o_sig, wo = _mm_nt(xn2, wt, qk_w + v_w, v_w, _sigmoid, F32, sides=(w_o[0],))
    gates, w_pu, w_pg = _mm_nt(xn2, wt, gates_off, LANES, _identity, F32,
                               sides=(w_ple_up[0], w_ple_gate[0]))
    gate_bias = jnp.pad(jnp.concatenate([b_igate[0], b_fgate[0]]), (0, LANES - 2 * N_HEADS)).reshape(1, LANES)

    buf = jnp.pad(state_mlstm_conv[0], ((0, 0), (0, sseq - (CONV_W - 1)), (0, 0))).reshape(ms, qk_w)
    q_all, k_all = _conv(qk_pre, buf, w_conv[0], b_conv, mp, seq, sseq)
    sample = (q_all, k_all, v_a, o_sig, gates, gate_bias, g_head,
              state_mlstm_C[0], state_mlstm_n[0].reshape(nbs, N_HEADS, 1, DK),
              jnp.broadcast_to(state_mlstm_m[0][:, :, None, None], (nbs, N_HEADS, 1, LANES)), mp, sseq)
    n_units = nbs // (BF16_ROWS // sseq) * N_HEADS
    uv_gelu, *part = _mm_nt_mlstm(xn2, wt, tail_off, 2 * d_b, _gelu, F32, sample, 0)
    ab_sig, ha_s, c_s, n_s, m_s = _mm_nt_mlstm(xn2, wt, tail_off + 2 * d_b, 2 * d, _sigmoid, F32,
                                               sample, n_units // 2, prev=part)
    ha_p, c_p, n_p, m_p = _mlstm_prompt(q_all, k_all, v_a, o_sig, gates, gate_bias, g_head, nbp, seq)

    t_idx = jnp.arange(GMLP_CHUNK) % sseq
    src = t_idx[:, None] - jnp.arange(sseq)[None, :]
    w_small = w_spatial[0][:, :sseq, :sseq]
    coef = jnp.where(src[None] >= 0, w_small[:, t_idx[:, None], jnp.maximum(src, 0)], 0.0)
    coef = coef.transpose(1, 0, 2).reshape(GMLP_CHUNK, N_GROUPS * sseq)
    bias_s = b_spatial[0][:, t_idx].T
    sg_all, vg_s = _gmlp(uv_gelu, g_ln_v, b_ln_v, w_spatial[0], b_spatial[0].T, coef, bias_s, mp, ms, sseq)

    h2, xn3 = _merge(ha_p, ha_s, sg_all, ab_sig, h1, wa, wb, wo, g_mix_post, g_ffn2_pre)
    hid2, wd2 = _ffn_up(xn3, w_ffn2_gate[0], w_ffn2_up[0], w_ffn2_down[0])
    h3, xn4 = _ffn_down(hid2, wd2, (h2,), g_ffn2_post, g_ple_pre)
    y_p, y_s = _ple(xn4, p_all, h3, w_pg, w_pu, g_ple_post, mp)

    keep = CONV_W - 1
    conv_prompt = jnp.stack([qk_pre[(b + 1) * seq - keep:(b + 1) * seq] for b in range(nbp)])
    conv_sample = qk_pre[mp:].reshape(nbs, sseq, qk_w)[:, sseq - keep:]
    return (y_p.reshape(nbp, seq, d), y_s.reshape(nbs, sseq, d),
            conv_prompt[None], c_p[None], n_p.reshape(1, nbp, N_HEADS, DK), m_p[:, :, 0, 0][None],
            conv_sample[None], c_s[None], n_s.reshape(1, nbs, N_HEADS, DK), m_s[:, :, 0, 0][None],
            vg_s.reshape(1, nbs, sseq, d_b))
```

```python
import functools

import jax
import jax.numpy as jnp
from jax import lax
from jax.experimental import pallas as pl
from jax.experimental.pallas import tpu as pltpu

F32 = jnp.float32
BF16 = jnp.bfloat16
EPS = 1e-6

N_HEADS = 4
DK = 256
DV = 512
CONV_W = 4
N_GROUPS = 4
GMLP_CHUNK = 128
GMLP_TILE = 256
LANES = 128
SUBLANES = 8
BF16_ROWS = 16
MLSTM_CHUNK = 256
VMEM_LIMIT = 54 * 1024 * 1024
VMEM_LIMIT_BIG = 60 * 1024 * 1024


def _params(*sem, vmem=VMEM_LIMIT):
    return pltpu.CompilerParams(dimension_semantics=sem, vmem_limit_bytes=vmem)


def _rms(x, g):
    return x * lax.rsqrt(jnp.mean(x * x, axis=-1, keepdims=True) + EPS) * g


def _gelu(x):
    return 0.5 * x * (1.0 + lax.erf(x * 0.7071067811865476))


def _sigmoid(x):
    return jax.nn.sigmoid(x)


def _identity(x):
    return x


def _log_sigmoid(x):
    return jnp.minimum(x, 0.0) - jnp.log1p(jnp.exp(-jnp.abs(x)))


def _resident(shape):
    return pl.BlockSpec(shape, lambda *_: (0,) * len(shape), pipeline_mode=pl.Buffered(1))


def _split_specs(tm, width, n_p):
    return [pl.BlockSpec((tm, width), lambda i: (jnp.minimum(i, n_p - 1), 0)),
            pl.BlockSpec((tm, width), lambda i: (jnp.maximum(i - n_p, 0), 0))]


def _head_gates(g, head):
    lane = lax.broadcasted_iota(jnp.int32, g.shape, 1)
    ig = jnp.sum(jnp.where(lane == head, g, 0.0), axis=1, keepdims=True)
    fg = jnp.sum(jnp.where(lane == head + N_HEADS, g, 0.0), axis=1, keepdims=True)
    return ig, fg


def _column_to_row(col, n):
    r_i = lax.broadcasted_iota(jnp.int32, (n, n), 0)
    c_i = lax.broadcasted_iota(jnp.int32, (n, n), 1)
    return jnp.sum(jnp.where(r_i == c_i, jnp.broadcast_to(col, (n, n)), 0.0), axis=0, keepdims=True)


def _rms_cast_kernel(xp_ref, xs_ref, g_ref, o_ref, *, n_p):
    i = pl.program_id(0)

    @pl.when(i < n_p)
    def _():
        o_ref[...] = _rms(xp_ref[...], g_ref[...]).astype(o_ref.dtype)

    @pl.when(i >= n_p)
    def _():
        o_ref[...] = _rms(xs_ref[...], g_ref[...]).astype(o_ref.dtype)


def _rms_cast(x_p, x_s, g, tm=512):
    (mp, d), ms = x_p.shape, x_s.shape[0]
    n_p = mp // tm
    return pl.pallas_call(
        functools.partial(_rms_cast_kernel, n_p=n_p),
        grid=((mp + ms) // tm,),
        in_specs=_split_specs(tm, d, n_p) + [pl.BlockSpec((1, d), lambda i: (0, 0))],
        out_specs=pl.BlockSpec((tm, d), lambda i: (i, 0)),
        out_shape=jax.ShapeDtypeStruct((mp + ms, d), BF16),
        compiler_params=_params("arbitrary"),
        name="rms_cast",
    )(x_p, x_s, g)


def _side_specs(side, n_steps, n_inner):
    rows, cols = side.shape
    slab = rows // n_steps
    assert slab * n_steps == rows and slab % BF16_ROWS == 0
    spec = pl.BlockSpec((slab, cols), lambda j, i: (j * n_inner + i, 0))
    return spec, jax.ShapeDtypeStruct((rows, cols), BF16)


def _ffn_up_kernel(x_ref, wg_ref, wu_ref, side_ref, o_ref, side_out, wgb, wub):
    @pl.when(pl.program_id(1) == 0)
    def _():
        wgb[...] = wg_ref[...].astype(BF16)
        wub[...] = wu_ref[...].astype(BF16)

    side_out[...] = side_ref[...].astype(BF16)
    x = x_ref[...]
    g = jnp.dot(x, wgb[...], preferred_element_type=F32)
    u = jnp.dot(x, wub[...], preferred_element_type=F32)
    o_ref[...] = (g * _sigmoid(g) * u).astype(o_ref.dtype)


def _ffn_up(xn, wg, wu, side, tm=2176, tf=512):
    m, d = xn.shape
    f = wg.shape[1]
    grid = (f // tf, m // tm)
    side_spec, side_shape = _side_specs(side, grid[0] * grid[1], grid[1])
    return pl.pallas_call(
        _ffn_up_kernel,
        grid=grid,
        in_specs=[pl.BlockSpec((tm, d), lambda j, i: (i, 0)),
                  pl.BlockSpec((d, tf), lambda j, i: (0, j)),
                  pl.BlockSpec((d, tf), lambda j, i: (0, j)),
                  side_spec],
        out_specs=[pl.BlockSpec((tm, tf), lambda j, i: (i, j)), side_spec],
        out_shape=[jax.ShapeDtypeStruct((m, f), BF16), side_shape],
        scratch_shapes=[pltpu.VMEM((d, tf), BF16), pltpu.VMEM((d, tf), BF16)],
        compiler_params=_params("arbitrary", "arbitrary", vmem=VMEM_LIMIT_BIG),
        name="ffn_up",
    )(xn, wg, wu, side)


def _mm_nt_kernel(*refs, act, n_side):
    x_ref, w_ref = refs[:2]
    o_ref = refs[2 + n_side]
    wb = refs[-1]
    for side_ref, side_out in zip(refs[2:2 + n_side], refs[3 + n_side:3 + 2 * n_side]):
        side_out[...] = side_ref[...].astype(BF16)

    @pl.when(pl.program_id(1) == 0)
    def _():
        wb[...] = w_ref[...].astype(BF16)

    y = lax.dot_general(x_ref[...], wb[...], (((1,), (1,)), ((), ())), preferred_element_type=F32)
    o_ref[...] = act(y).astype(o_ref.dtype)


def _mm_nt(x, wt, row_off, n_out, act, out_dtype, sides=(), tm=1088, tn=1024):
    m, k = x.shape
    tn = min(tn, n_out)
    grid = (n_out // tn, m // tm)
    assert grid[0] * tn == n_out and row_off % SUBLANES == 0
    if row_off % tn == 0:
        w_spec = pl.BlockSpec((tn, k), lambda j, i: (row_off // tn + j, 0))
    else:
        w_spec = pl.BlockSpec((pl.Element(tn), pl.Element(k)),
                              lambda j, i: (pl.multiple_of(row_off + j * tn, SUBLANES), 0))
    in_specs = [pl.BlockSpec((tm, k), lambda j, i: (i, 0)), w_spec]
    out_specs = [pl.BlockSpec((tm, tn), lambda j, i: (i, j))]
    out_shape = [jax.ShapeDtypeStruct((m, n_out), out_dtype)]
    args = [x, wt]
    for side in sides:
        side_spec, side_shape = _side_specs(side, grid[0] * grid[1], grid[1])
        in_specs.append(side_spec)
        out_specs.append(side_spec)
        out_shape.append(side_shape)
        args.append(side)
    out = pl.pallas_call(
        functools.partial(_mm_nt_kernel, act=act, n_side=len(sides)),
        grid=grid,
        in_specs=in_specs,
        out_specs=out_specs,
        out_shape=out_shape,
        scratch_shapes=[pltpu.VMEM((tn, k), BF16)],
        compiler_params=_params("arbitrary", "arbitrary"),
        name="mm_nt",
    )(*args)
    return out if sides else out[0]


def _ffn_down_kernel(*refs, n_p):
    if n_p is None:
        h_ref, wd_ref, x_ref, gpost_ref, gnext_ref, hout_ref, xn_ref = refs
        x = x_ref[...]
    else:
        h_ref, wd_ref, xp_ref, xs_ref, gpost_ref, gnext_ref, hout_ref, xn_ref = refs
        x = jnp.where(pl.program_id(0) < n_p, xp_ref[...], xs_ref[...])
    y = jnp.dot(h_ref[...], wd_ref[...], preferred_element_type=F32)
    h = x + 0.5 * _rms(y, gpost_ref[...])
    hout_ref[...] = h
    xn_ref[...] = _rms(h, gnext_ref[...]).astype(xn_ref.dtype)


def _ffn_down(hid, wd, x_parts, g_post, g_next, tm=256):
    m, f = hid.shape
    d = wd.shape[1]
    row = lambda i: (i, 0)
    if len(x_parts) == 1:
        n_p, x_specs = None, [pl.BlockSpec((tm, d), row)]
    else:
        n_p = x_parts[0].shape[0] // tm
        x_specs = _split_specs(tm, d, n_p)
    return pl.pallas_call(
        functools.partial(_ffn_down_kernel, n_p=n_p),
        grid=(m // tm,),
        in_specs=[pl.BlockSpec((tm, f), row), _resident((f, d))] + x_specs
                 + [_resident((1, d)), _resident((1, d))],
        out_specs=[pl.BlockSpec((tm, d), row), pl.BlockSpec((tm, d), row)],
        out_shape=[jax.ShapeDtypeStruct((m, d), F32), jax.ShapeDtypeStruct((m, d), BF16)],
        compiler_params=_params("arbitrary"),
        name="ffn_down",
    )(hid, wd, *x_parts, g_post, g_next)


def _merge_kernel(hap_ref, has_ref, sg_ref, gates_a_ref, gates_b_ref, h_ref, wa_ref, wb_ref, wo_ref,
                  gpost_ref, gnext_ref, hout_ref, xn_ref, *, n_p):
    ha = jnp.where(pl.program_id(0) < n_p, hap_ref[...], has_ref[...])
    ya = jnp.dot(ha, wa_ref[...], preferred_element_type=F32)
    yb = jnp.dot(sg_ref[...], wb_ref[...], preferred_element_type=F32)
    mixin = (gates_a_ref[...] * ya + gates_b_ref[...] * yb).astype(BF16)
    mix = jnp.dot(mixin, wo_ref[...], preferred_element_type=F32)
    h = h_ref[...] + _rms(mix, gpost_ref[...])
    hout_ref[...] = h
    xn_ref[...] = _rms(h, gnext_ref[...]).astype(xn_ref.dtype)


def _merge(ha_p, ha_s, sg, ab_sig, h, wa, wb, wo, g_post, g_next, tm=256):
    m, d = h.shape
    n_p = ha_p.shape[0] // tm
    row = lambda i: (i, 0)
    return pl.pallas_call(
        functools.partial(_merge_kernel, n_p=n_p),
        grid=(m // tm,),
        in_specs=_split_specs(tm, d, n_p)
                 + [pl.BlockSpec((tm, d), row),
                    pl.BlockSpec((tm, d), lambda i: (i, 0)), pl.BlockSpec((tm, d), lambda i: (i, 1)),
                    pl.BlockSpec((tm, d), row),
                    _resident((d, d)), _resident((d, d)), _resident((d, d)),
                    _resident((1, d)), _resident((1, d))],
        out_specs=[pl.BlockSpec((tm, d), row), pl.BlockSpec((tm, d), row)],
        out_shape=[jax.ShapeDtypeStruct((m, d), F32), jax.ShapeDtypeStruct((m, d), BF16)],
        compiler_params=_params("arbitrary"),
        name="merge",
    )(ha_p, ha_s, sg, ab_sig, ab_sig, h, wa, wb, wo, g_post, g_next)


def _ple_kernel(xn_ref, p_ref, h_ref, wg_ref, wu_ref, gpost_ref, outp_ref, outs_ref, *, n_p):
    gate = _sigmoid(jnp.dot(xn_ref[...], wg_ref[...], preferred_element_type=F32))
    up = jnp.dot(p_ref[...].astype(BF16), wu_ref[...], preferred_element_type=F32)
    out = h_ref[...] + _rms(gate * up, gpost_ref[...])
    i = pl.program_id(0)

    @pl.when(i < n_p)
    def _():
        outp_ref[...] = out

    @pl.when(i >= n_p)
    def _():
        outs_ref[...] = out


def _ple(xn, p, h, wg, wu, g_post, mp, tm=256):
    m, d = h.shape
    dp = p.shape[1]
    n_p = mp // tm
    row = lambda i: (i, 0)
    return pl.pallas_call(
        functools.partial(_ple_kernel, n_p=n_p),
        grid=(m // tm,),
        in_specs=[pl.BlockSpec((tm, d), row), pl.BlockSpec((tm, dp), row), pl.BlockSpec((tm, d), row),
                  _resident((d, d)), _resident((dp, d)), _resident((1, d))],
        out_specs=_split_specs(tm, d, n_p),
        out_shape=[jax.ShapeDtypeStruct((mp, d), F32), jax.ShapeDtypeStruct((m - mp, d), F32)],
        compiler_params=_params("arbitrary"),
        name="ple",
    )(xn, p, h, wg, wu, g_post)


def _conv_kernel(x_ref, buf_ref, w_ref, b_ref, q_ref, k_ref, prev, *, ts, n_p, tiles_per_seq, sample_len):
    i = pl.program_id(0)
    w = w_ref[...]
    c = x_ref.shape[1]

    def finish(y):
        s = y * _sigmoid(y)
        q_ref[...] = s[:, :c // 2].astype(q_ref.dtype)
        k_ref[...] = (s[:, c // 2:] * (DK ** -0.5)).astype(k_ref.dtype)

    @pl.when(i < n_p)
    def _():
        @pl.when(lax.rem(i, tiles_per_seq) == 0)
        def _():
            prev[...] = jnp.zeros_like(prev)

        x = x_ref[...]
        tail = prev[...]
        y = b_ref[...] + x * w[CONV_W - 1:CONV_W, :]
        rowid = lax.broadcasted_iota(jnp.int32, (SUBLANES, c), 0)
        for j in range(1, CONV_W):
            xr = pltpu.roll(x, j, axis=0)
            tr = pltpu.roll(tail, j, axis=0)
            top = jnp.where(rowid < j, tr, xr[0:SUBLANES])
            xs = jnp.concatenate([top, xr[SUBLANES:]], axis=0)
            y = y + xs * w[CONV_W - 1 - j:CONV_W - j, :]
        prev[...] = x[ts - SUBLANES:ts]
        finish(y)

    @pl.when(i >= n_p)
    def _():
        x = x_ref[...]
        buf = buf_ref[...]
        t = jnp.bitwise_and(lax.broadcasted_iota(jnp.int32, (ts, c), 0), sample_len - 1)
        y = b_ref[...] + x * w[CONV_W - 1:CONV_W, :]
        for j in range(1, CONV_W):
            back = (j - (CONV_W - 1)) % ts
            from_buf = buf if back == 0 else pltpu.roll(buf, back, axis=0)
            xs = jnp.where(t >= j, pltpu.roll(x, j, axis=0), from_buf)
            y = y + xs * w[CONV_W - 1 - j:CONV_W - j, :]
        finish(y)


def _conv(x, buf, w, b, mp, seq, sample_len, ts=256):
    m, c = x.shape
    n_p = mp // ts
    assert sample_len & (sample_len - 1) == 0 and sample_len >= CONV_W - 1 and ts % sample_len == 0
    kern = functools.partial(_conv_kernel, ts=ts, n_p=n_p, tiles_per_seq=seq // ts, sample_len=sample_len)
    row = lambda i: (i, 0)
    return pl.pallas_call(
        kern,
        grid=(m // ts,),
        in_specs=[pl.BlockSpec((ts, c), row),
                  pl.BlockSpec((ts, c), lambda i: (jnp.maximum(i - n_p, 0), 0)),
                  pl.BlockSpec((CONV_W, c), lambda i: (0, 0)),
                  pl.BlockSpec((1, c), lambda i: (0, 0))],
        out_specs=[pl.BlockSpec((ts, c // 2), row), pl.BlockSpec((ts, c // 2), row)],
        out_shape=[jax.ShapeDtypeStruct((m, c // 2), BF16), jax.ShapeDtypeStruct((m, c // 2), BF16)],
        scratch_shapes=[pltpu.VMEM((SUBLANES, c), F32)],
        compiler_params=_params("arbitrary"),
        name="conv",
    )(x, buf, w, b)


def _head_out(hh, gh, o):
    hn = hh * lax.rsqrt(jnp.mean(hh * hh, axis=1, keepdims=True) + EPS) * gh
    return (o * hn).astype(BF16)


def _mlstm_prompt_kernel(q_ref, k_ref, v_ref, o_ref, g_ref, gb_ref, gh_ref,
                         h_out, c_out, n_out, m_out, c_s, n_s, m_s, *, chunk):
    L = chunk
    ci = pl.program_id(1)

    @pl.when(ci == 0)
    def _():
        c_s[...] = jnp.zeros_like(c_s)
        n_s[...] = jnp.zeros_like(n_s)
        m_s[...] = jnp.zeros_like(m_s)

    g = g_ref[...] + gb_ref[...]
    bc = _log_sigmoid(g)
    rowi = lax.broadcasted_iota(jnp.int32, (L, LANES), 0)
    sh = 1
    while sh < L:
        bc = bc + jnp.where(rowi >= sh, pltpu.roll(bc, sh, axis=0), 0.0)
        sh *= 2
    a_all = pltpu.roll(g, N_HEADS, axis=1) - bc
    a_t = a_all.T
    r_i = lax.broadcasted_iota(jnp.int32, (L, L), 0)
    c_i = lax.broadcasted_iota(jnp.int32, (L, L), 1)
    causal = r_i >= c_i

    for h in range(N_HEADS):
        lane = N_HEADS + h
        bcum = bc[:, lane:lane + 1]
        a_col = a_all[:, lane:lane + 1]
        a_row = a_t[lane:lane + 1, :]
        m_prev = m_s[h][0:1, 0:1]
        m_in = bcum + m_prev
        d = jnp.where(causal, bcum + a_row, -jnp.inf)
        m_t = jnp.maximum(m_in, jnp.max(d, axis=1, keepdims=True))
        p = jnp.exp(d - m_t)

        qb = q_ref[:, h * DK:(h + 1) * DK]
        kb = k_ref[:, h * DK:(h + 1) * DK]
        vb = v_ref[:, h * DV:(h + 1) * DV]
        s = lax.dot_general(qb, kb, (((1,), (1,)), ((), ())), preferred_element_type=F32) * p
        w_prev = jnp.exp(m_in - m_t)
        c_old = c_s[h]
        n_old = n_s[h]
        num = (jnp.dot(s.astype(BF16), vb, preferred_element_type=F32)
               + w_prev * jnp.dot(qb, c_old.astype(BF16), preferred_element_type=F32))
        den = (jnp.sum(s, axis=1, keepdims=True)
               + w_prev * jnp.sum(qb.astype(F32) * n_old, axis=1, keepdims=True))
        hh = num / jnp.maximum(jnp.abs(den), jnp.exp(-m_t))
        h_out[:, h * DV:(h + 1) * DV] = _head_out(hh, gh_ref[:, h * DV:(h + 1) * DV],
                                                  o_ref[:, h * DV:(h + 1) * DV])

        m_new = m_t[L - 1:L, :]
        b_last = bcum[L - 1:L, :]
        w_end = jnp.exp(b_last + a_col - m_new)
        decay = jnp.exp(b_last + m_prev - m_new)
        kw = kb.astype(F32) * w_end
        c_s[h] = decay * c_old + lax.dot_general(
            kw.astype(BF16), vb, (((0,), (0,)), ((), ())), preferred_element_type=F32)
        n_s[h] = decay * n_old + jnp.sum(kw, axis=0, keepdims=True)
        m_s[h] = jnp.broadcast_to(m_new, (1, LANES))

    @pl.when(ci == pl.num_programs(1) - 1)
    def _():
        c_out[...] = c_s[...]
        n_out[...] = n_s[...]
        m_out[...] = m_s[...]


def _mlstm_prompt(q, k, v, o, gates, gate_bias, g_head, nb, seq, chunk=MLSTM_CHUNK):
    L = chunk
    nc = seq // L
    rows = lambda b, c: (b * nc + c, 0)
    state = lambda b, c: (b, 0, 0, 0)
    return pl.pallas_call(
        functools.partial(_mlstm_prompt_kernel, chunk=L),
        grid=(nb, nc),
        in_specs=[pl.BlockSpec((L, N_HEADS * DK), rows), pl.BlockSpec((L, N_HEADS * DK), rows),
                  pl.BlockSpec((L, N_HEADS * DV), rows), pl.BlockSpec((L, N_HEADS * DV), rows),
                  pl.BlockSpec((L, LANES), rows),
                  pl.BlockSpec((1, LANES), lambda b, c: (0, 0)),
                  pl.BlockSpec((1, N_HEADS * DV), lambda b, c: (0, 0))],
        out_specs=[pl.BlockSpec((L, N_HEADS * DV), rows),
                   pl.BlockSpec((None, N_HEADS, DK, DV), state),
                   pl.BlockSpec((None, N_HEADS, 1, DK), state),
                   pl.BlockSpec((None, N_HEADS, 1, LANES), state)],
        out_shape=[jax.ShapeDtypeStruct((nb * seq, N_HEADS * DV), BF16),
                   jax.ShapeDtypeStruct((nb, N_HEADS, DK, DV), F32),
                   jax.ShapeDtypeStruct((nb, N_HEADS, 1, DK), F32),
                   jax.ShapeDtypeStruct((nb, N_HEADS, 1, LANES), F32)],
        scratch_shapes=[pltpu.VMEM((N_HEADS, DK, DV), F32), pltpu.VMEM((N_HEADS, 1, DK), F32),
                        pltpu.VMEM((N_HEADS, 1, LANES), F32)],
        compiler_params=_params("arbitrary", "arbitrary"),
        name="mlstm_prompt",
    )(q, k, v, o, gates, gate_bias, g_head)


def _mlstm_sample_unit(head, q_ref, k_ref, v_ref, o_ref, g_ref, gb_ref, gh_ref, c0_ref, n0_ref, m0_ref,
                       h_out, c_out, n_out, m_out, *, nb, steps):
    R = nb * steps
    log_steps = steps.bit_length() - 1

    ig, fg = _head_gates(g_ref[...] + gb_ref[...], head)
    lf = _log_sigmoid(fg)

    bc = jnp.broadcast_to(lf, (R, LANES))
    tt = jnp.bitwise_and(lax.broadcasted_iota(jnp.int32, (R, LANES), 0), steps - 1)
    sh = 1
    while sh < steps:
        bc = bc + jnp.where(tt >= sh, pltpu.roll(bc, sh, axis=0), 0.0)
        sh *= 2
    bcum = bc[:, 0:1]

    seq_of_row = lax.shift_right_logical(lax.broadcasted_iota(jnp.int32, (R, 1), 0), log_steps)

    def per_row(vals):
        out = jnp.broadcast_to(vals[0], (R, 1))
        for b in range(1, nb):
            out = jnp.where(seq_of_row == b, vals[b], out)
        return out

    m0 = [m0_ref[b][0:1, 0:1] for b in range(nb)]
    m_in = bcum + per_row(m0)
    a_col = ig - bcum
    a_row = _column_to_row(a_col, R)
    r_i = lax.broadcasted_iota(jnp.int32, (R, R), 0)
    c_i = lax.broadcasted_iota(jnp.int32, (R, R), 1)
    same_seq = lax.shift_right_logical(r_i, log_steps) == lax.shift_right_logical(c_i, log_steps)
    d = jnp.where(same_seq, jnp.where(r_i >= c_i, bcum + a_row, -jnp.inf), -jnp.inf)
    m_t = jnp.maximum(m_in, jnp.max(d, axis=1, keepdims=True))
    p = jnp.exp(d - m_t)

    qb = q_ref[...]
    kb = k_ref[...]
    vb = v_ref[...]
    s = lax.dot_general(qb, kb, (((1,), (1,)), ((), ())), preferred_element_type=F32) * p
    w_prev = jnp.exp(m_in - m_t)
    inter = jnp.zeros((R, DV), F32)
    n_rows = jnp.zeros((R, DK), F32)
    for b in range(nb):
        qc = jnp.dot(qb, c0_ref[b].astype(BF16), preferred_element_type=F32)
        inter = jnp.where(seq_of_row == b, qc, inter)
        n_rows = jnp.where(seq_of_row == b, n0_ref[b], n_rows)
    num = jnp.dot(s.astype(BF16), vb, preferred_element_type=F32) + w_prev * inter
    den = (jnp.sum(s, axis=1, keepdims=True)
           + w_prev * jnp.sum(qb.astype(F32) * n_rows, axis=1, keepdims=True))
    hh = num / jnp.maximum(jnp.abs(den), jnp.exp(-m_t))
    h_out[...] = _head_out(hh, gh_ref[...], o_ref[...])

    last = [b * steps + steps - 1 for b in range(nb)]
    m_new = [m_t[r:r + 1, :] for r in last]
    b_last = [bcum[r:r + 1, :] for r in last]
    w_end = jnp.exp(per_row(b_last) + a_col - per_row(m_new))
    kw = kb.astype(F32) * w_end
    for b in range(nb):
        decay = jnp.exp(b_last[b] + m0[b] - m_new[b])
        kw_b = jnp.where(seq_of_row == b, kw, 0.0)
        c_out[b] = decay * c0_ref[b] + lax.dot_general(
            kw_b.astype(BF16), vb, (((0,), (0,)), ((), ())), preferred_element_type=F32)
        n_out[b] = decay * n0_ref[b] + jnp.sum(kw_b, axis=0, keepdims=True)
        m_out[b] = jnp.broadcast_to(m_new[b], (1, LANES))


def _mm_nt_mlstm_kernel(*refs, act, unit0, n_inner, nb, steps, n_prev):
    x_ref, w_ref = refs[:2]
    unit_in = refs[2:12]
    o_ref = refs[12 + n_prev]
    unit_out = refs[13 + n_prev:17 + n_prev]
    wb = refs[17 + n_prev]

    @pl.when(pl.program_id(1) == 0)
    def _():
        wb[...] = w_ref[...].astype(BF16)

    unit = unit0 + pl.program_id(0) * n_inner + pl.program_id(1)
    _mlstm_sample_unit(lax.rem(unit, N_HEADS), *unit_in, *unit_out, nb=nb, steps=steps)
    y = lax.dot_general(x_ref[...], wb[...], (((1,), (1,)), ((), ())), preferred_element_type=F32)
    o_ref[...] = act(y).astype(o_ref.dtype)


def _mm_nt_mlstm(x, wt, row_off, n_out, act, out_dtype, sample, unit0, prev=None, tm=1088, tn=512):
    *unit_args, row0, steps = sample
    m, k = x.shape
    grid = (n_out // tn, m // tm)
    nbs = unit_args[7].shape[0]
    nb = BF16_ROWS // steps
    R = nb * steps
    assert grid[0] * tn == n_out and row_off % SUBLANES == 0
    assert steps & (steps - 1) == 0 and nb * steps == BF16_ROWS and row0 % R == 0 and nbs % nb == 0
    assert 0 <= unit0 and unit0 + grid[0] * grid[1] <= (nbs // nb) * N_HEADS
    r0 = row0 // R

    def unit(j, i):
        u = unit0 + j * grid[1] + i
        return u // N_HEADS, lax.rem(u, N_HEADS)

    def rows(j, i):
        g, h = unit(j, i)
        return r0 + g, h

    def state(j, i):
        g, h = unit(j, i)
        return g, h, 0, 0

    in_specs = [pl.BlockSpec((tm, k), lambda j, i: (i, 0)),
                pl.BlockSpec((pl.Element(tn), pl.Element(k)),
                             lambda j, i: (pl.multiple_of(row_off + j * tn, SUBLANES), 0)),
                pl.BlockSpec((R, DK), rows), pl.BlockSpec((R, DK), rows),
                pl.BlockSpec((R, DV), rows), pl.BlockSpec((R, DV), rows),
                pl.BlockSpec((R, LANES), lambda j, i: (r0 + unit(j, i)[0], 0)),
                pl.BlockSpec((1, LANES), lambda j, i: (0, 0)),
                pl.BlockSpec((1, DV), lambda j, i: (0, unit(j, i)[1])),
                pl.BlockSpec((nb, None, DK, DV), state),
                pl.BlockSpec((nb, None, 1, DK), state),
                pl.BlockSpec((nb, None, 1, LANES), state)]
    args = [x, wt, *unit_args]
    aliases = {}
    if prev is not None:
        aliases = {len(args) + a: 1 + a for a in range(len(prev))}
        in_specs += [pl.BlockSpec(memory_space=pl.ANY)] * len(prev)
        args += list(prev)
    return pl.pallas_call(
        functools.partial(_mm_nt_mlstm_kernel, act=act, unit0=unit0, n_inner=grid[1], nb=nb, steps=steps,
                          n_prev=0 if prev is None else len(prev)),
        grid=grid,
        in_specs=in_specs,
        out_specs=[pl.BlockSpec((tm, tn), lambda j, i: (i, j)),
                   pl.BlockSpec((R, DV), lambda j, i: unit(j, i)),
                   pl.BlockSpec((nb, None, DK, DV), state),
                   pl.BlockSpec((nb, None, 1, DK), state),
                   pl.BlockSpec((nb, None, 1, LANES), state)],
        out_shape=[jax.ShapeDtypeStruct((m, n_out), out_dtype),
                   jax.ShapeDtypeStruct((nbs * steps, N_HEADS * DV), BF16),
                   jax.ShapeDtypeStruct((nbs, N_HEADS, DK, DV), F32),
                   jax.ShapeDtypeStruct((nbs, N_HEADS, 1, DK), F32),
                   jax.ShapeDtypeStruct((nbs, N_HEADS, 1, LANES), F32)],
        scratch_shapes=[pltpu.VMEM((tn, k), BF16)],
        input_output_aliases=aliases,
        compiler_params=_params("arbitrary", "arbitrary"),
        name="mm_nt_mlstm",
    )(*args)


def _gmlp_kernel(u_ref, v_ref, gln_ref, bln_ref, wsp_ref, bsp_ref, coef_ref, bias_ref,
                 out_ref, vg_ref, *, n_prompt_tiles, sample_len):
    x = v_ref[...]
    mu = jnp.mean(x, axis=-1, keepdims=True)
    xc = x - mu
    var = jnp.mean(xc * xc, axis=-1, keepdims=True)
    vg = xc * lax.rsqrt(var + EPS) * gln_ref[...] + bln_ref[...]
    u = u_ref[...]
    rows, width = x.shape
    gw = width // N_GROUPS
    is_prompt = pl.program_id(0) < n_prompt_tiles

    @pl.when(is_prompt)
    def _():
        r_i = lax.broadcasted_iota(jnp.int32, (GMLP_CHUNK, GMLP_CHUNK), 0)
        c_i = lax.broadcasted_iota(jnp.int32, (GMLP_CHUNK, GMLP_CHUNK), 1)
        for g in range(N_GROUPS):
            w = jnp.where(r_i >= c_i, wsp_ref[g], 0.0).astype(BF16)
            sl = slice(g * gw, (g + 1) * gw)
            for c in range(rows // GMLP_CHUNK):
                rs = slice(c * GMLP_CHUNK, (c + 1) * GMLP_CHUNK)
                mixed = (jnp.dot(w, vg[rs, sl].astype(BF16), preferred_element_type=F32)
                         + bsp_ref[:, g:g + 1])
                out_ref[rs, sl] = (u[rs, sl] * mixed).astype(out_ref.dtype)

    @pl.when(jnp.logical_not(is_prompt))
    def _():
        vg_ref[...] = vg
        for g in range(N_GROUPS):
            sl = slice(g * gw, (g + 1) * gw)
            vgg = vg[:, sl]
            acc = bias_ref[:, g:g + 1] + coef_ref[:, g * sample_len:g * sample_len + 1] * vgg
            for j in range(1, sample_len):
                cj = coef_ref[:, g * sample_len + j:g * sample_len + j + 1]
                acc = acc + cj * pltpu.roll(vgg, j, axis=0)
            out_ref[:, sl] = (u[:, sl] * acc).astype(out_ref.dtype)


def _gmlp(uv, g_ln, b_ln, w_sp, b_sp_t, coef, bias, n_prompt_rows, n_sample_rows, sample_len):
    width = g_ln.shape[1]
    tr = coef.shape[0]
    assert tr % GMLP_CHUNK == 0
    n_pt = n_prompt_rows // tr
    n_st = n_sample_rows // tr
    kern = functools.partial(_gmlp_kernel, n_prompt_tiles=n_pt, sample_len=sample_len)
    const2 = lambda i: (0, 0)
    return pl.pallas_call(
        kern,
        grid=(n_pt + n_st,),
        in_specs=[pl.BlockSpec((tr, width), lambda i: (i, 0)),
                  pl.BlockSpec((tr, width), lambda i: (i, 1)),
                  pl.BlockSpec((1, width), const2), pl.BlockSpec((1, width), const2),
                  pl.BlockSpec((N_GROUPS, GMLP_CHUNK, GMLP_CHUNK), lambda i: (0, 0, 0)),
                  pl.BlockSpec((GMLP_CHUNK, N_GROUPS), const2),
                  pl.BlockSpec((tr, N_GROUPS * sample_len), const2),
                  pl.BlockSpec((tr, N_GROUPS), const2)],
        out_specs=[pl.BlockSpec((tr, width), lambda i: (i, 0)),
                   pl.BlockSpec((tr, width), lambda i: (jnp.maximum(i - n_pt, 0), 0))],
        out_shape=[jax.ShapeDtypeStruct((n_prompt_rows + n_sample_rows, width), BF16),
                   jax.ShapeDtypeStruct((n_sample_rows, width), F32)],
        compiler_params=_params("arbitrary"),
        name="gmlp",
    )(uv, uv, g_ln, b_ln, w_sp, b_sp_t, coef, bias)


def kernel(x_prompt, x_sample, p_prompt, p_sample, state_mlstm_conv, state_mlstm_C, state_mlstm_n, state_mlstm_m, g_ffn1_pre, w_ffn1_gate, w_ffn1_up, w_ffn1_down, g_ffn1_post, g_mix_pre, w_in, w_conv, b_conv, b_igate, b_fgate, g_head, w_a_out, g_ln_v, b_ln_v, w_spatial, b_spatial, w_b_out, w_o, g_mix_post, g_ffn2_pre, w_ffn2_gate, w_ffn2_up, w_ffn2_down, g_ffn2_post, g_ple_pre, w_ple_gate, w_ple_up, g_ple_post):
    assert w_in.shape[0] == 1, "single layer"
    nbp, seq, d = x_prompt.shape
    nbs, sseq, _ = x_sample.shape
    mp, ms = nbp * seq, nbs * sseq
    qk_w = 2 * N_HEADS * DK
    v_w = N_HEADS * DV
    d_b = g_ln_v.shape[1]
    gates_off = qk_w + 2 * v_w
    tail_off = gates_off + 2 * N_HEADS

    x_p, x_s = x_prompt.reshape(mp, d), x_sample.reshape(ms, d)
    p_all = jnp.concatenate([p_prompt[0].reshape(mp, -1), p_sample[0].reshape(ms, -1)], axis=0)

    xn1 = _rms_cast(x_p, x_s, g_ffn1_pre)
    hid1, wd1 = _ffn_up(xn1, w_ffn1_gate[0], w_ffn1_up[0], w_ffn1_down[0])
    h1, xn2 = _ffn_down(hid1, wd1, (x_p, x_s), g_ffn1_post, g_mix_pre)

    wt = jnp.swapaxes(w_in, 1, 2)[0]
    qk_pre, wa = _mm_nt(xn2, wt, 0, qk_w, _identity, F32, sides=(w_a_out[0],))
    v_a, wb = _mm_nt(xn2, wt, qk_w, v_w, _identity, BF16, sides=(w_b_out[0],))
    o_sig, wo = _mm_nt(xn2, wt, qk_w + v_w, v_w, _sigmoid, F32, sides=(w_o[0],))
    gates, w_pu, w_pg = _mm_nt(xn2, wt, gates_off, LANES, _identity, F32,
                               sides=(w_ple_up[0], w_ple_gate[0]))
    gate_bias = jnp.pad(jnp.concatenate([b_igate[0], b_fgate[0]]), (0, LANES - 2 * N_HEADS)).reshape(1, LANES)

    buf = jnp.pad(state_mlstm_conv[0], ((0, 0), (0, sseq - (CONV_W - 1)), (0, 0))).reshape(ms, qk_w)
    q_all, k_all = _conv(qk_pre, buf, w_conv[0], b_conv, mp, seq, sseq)
    sample = (q_all, k_all, v_a, o_sig, gates, gate_bias, g_head,
              state_mlstm_C[0], state_mlstm_n[0].reshape(nbs, N_HEADS, 1, DK),
              jnp.broadcast_to(state_mlstm_m[0][:, :, None, None], (nbs, N_HEADS, 1, LANES)), mp, sseq)
    n_units = nbs // (BF16_ROWS // sseq) * N_HEADS
    uv_gelu, *part = _mm_nt_mlstm(xn2, wt, tail_off, 2 * d_b, _gelu, F32, sample, 0)
    ab_sig, ha_s, c_s, n_s, m_s = _mm_nt_mlstm(xn2, wt, tail_off + 2 * d_b, 2 * d, _sigmoid, F32,
                                               sample, n_units // 2, prev=part)
    ha_p, c_p, n_p, m_p = _mlstm_prompt(q_all, k_all, v_a, o_sig, gates, gate_bias, g_head, nbp, seq)

    t_idx = jnp.arange(GMLP_TILE) % sseq
    src = t_idx[:, None] - jnp.arange(sseq)[None, :]
    w_small = w_spatial[0][:, :sseq, :sseq]
    coef = jnp.where(src[None] >= 0, w_small[:, t_idx[:, None], jnp.maximum(src, 0)], 0.0)
    coef = coef.transpose(1, 0, 2).reshape(GMLP_TILE, N_GROUPS * sseq)
    bias_s = b_spatial[0][:, t_idx].T
    sg_all, vg_s = _gmlp(uv_gelu, g_ln_v, b_ln_v, w_spatial[0], b_spatial[0].T, coef, bias_s, mp, ms, sseq)

    h2, xn3 = _merge(ha_p, ha_s, sg_all, ab_sig, h1, wa, wb, wo, g_mix_post, g_ffn2_pre)
    hid2, wd2 = _ffn_up(xn3, w_ffn2_gate[0], w_ffn2_up[0], w_ffn2_down[0])
    h3, xn4 = _ffn_down(hid2, wd2, (h2,), g_ffn2_post, g_ple_pre)
    y_p, y_s = _ple(xn4, p_all, h3, w_pg, w_pu, g_ple_post, mp)

    keep = CONV_W - 1
    conv_prompt = jnp.stack([qk_pre[(b + 1) * seq - keep:(b + 1) * seq] for b in range(nbp)])
    conv_sample = qk_pre[mp:].reshape(nbs, sseq, qk_w)[:, sseq - keep:]
    return (y_p.reshape(nbp, seq, d), y_s.reshape(nbs, sseq, d),
            conv_prompt[None], c_p[None], n_p.reshape(1, nbp, N_HEADS, DK), m_p[:, :, 0, 0][None],
            conv_sample[None], c_s[None], n_s.reshape(1, nbs, N_HEADS, DK), m_s[:, :, 0, 0][None],
            vg_s.reshape(1, nbs, sseq, d_b))
```
